```python
import math
import jax
import jax.numpy as jnp
from jax import lax
import numpy as np

D_MODEL = 2048
BATCH = 2
SEQ = 4096
DEPTH = 4

HEAD_DIM = 128
N_HEADS_A = 4
N_HEADS_B = 4
N_HEADS_C = 4
N_HEADS_D = 4
N_HEADS = N_HEADS_A + N_HEADS_B + N_HEADS_C + N_HEADS_D
MIX_WIDTH = N_HEADS * HEAD_DIM
ROPE_THETA = 500000.0
ROPE_FRACTION = 4
Q_BLOCK = 128

CMP_LEN = 32
CMP_STRIDE = 16
CMP_HIDDEN = 256
SLC_LEN = 64
SLC_TOPK = 16
WIN_LEN = 512
FORCE_BONUS = 1000.0

DILATED_PATTERNS = ((128, 1), (512, 4), (2048, 16))

DIFF_QK_DIM = HEAD_DIM // 2

PEER_HEADS = 8
PEER_NKEYS = 128
PEER_EXPERTS = PEER_NKEYS * PEER_NKEYS
PEER_QDIM = 256
PEER_TOPK = 16
PEER_CHUNK = 128

LN_EPS = 1e-5
RMS_EPS = 1e-6
TINY = 1e-30
DEEPNORM_ALPHA = (2 * DEPTH) ** 0.25
DEEPNORM_BETA = (8 * DEPTH) ** -0.25

IN_WIDTHS = (
    N_HEADS_A * HEAD_DIM, HEAD_DIM, HEAD_DIM, HEAD_DIM, HEAD_DIM, HEAD_DIM, HEAD_DIM, 3 * N_HEADS_A,
    N_HEADS_B * HEAD_DIM, N_HEADS_B * HEAD_DIM, N_HEADS_B * HEAD_DIM,
    2 * N_HEADS_C * DIFF_QK_DIM, 2 * N_HEADS_C * DIFF_QK_DIM, N_HEADS_C * HEAD_DIM,
    N_HEADS_D * HEAD_DIM, N_HEADS_D * HEAD_DIM, N_HEADS_D * HEAD_DIM,
)
IN_WIDTH = sum(IN_WIDTHS)

kernel_name = 'hybrid_nsa_dilated_diff_stickbreak_peer'


def layer_norm(x, g, b):
    xf = x.astype(jnp.float32)
    mu = jnp.mean(xf, -1, keepdims=True)
    var = jnp.mean(jnp.square(xf - mu), -1, keepdims=True)
    y = (xf - mu) * lax.rsqrt(var + LN_EPS) * g.astype(jnp.float32) + b.astype(jnp.float32)
    return y.astype(x.dtype)


def split_cols(t, widths):
    parts, start = [], 0
    for w in widths:
        parts.append(t[..., start:start + w])
        start += w
    return parts


def split_heads(t, n_heads):
    b, s, _ = t.shape
    return t.reshape(b, s, n_heads, -1).transpose(0, 2, 1, 3)


def partial_rope(x):
    s, dh = x.shape[2], x.shape[3]
    rot = dh // ROPE_FRACTION
    half = rot // 2
    inv_freq = ROPE_THETA ** (-jnp.arange(half, dtype=jnp.float32) / half)
    ang = jnp.arange(s, dtype=jnp.float32)[:, None] * inv_freq[None, :]
    cos = jnp.cos(ang).astype(x.dtype)
    sin = jnp.sin(ang).astype(x.dtype)
    x1, x2 = x[..., :half], x[..., half:rot]
    return jnp.concatenate([x1 * cos - x2 * sin, x1 * sin + x2 * cos, x[..., rot:]], axis=-1)


def rope_single(t):
    return partial_rope(t[:, None])[:, 0]


def to_query_blocks(q):
    b, h, s, d = q.shape
    return q.reshape(b, h, s // Q_BLOCK, Q_BLOCK, d).transpose(2, 0, 1, 3, 4)


def from_query_blocks(o):
    nq, b, h, qb, d = o.shape
    return o.transpose(1, 2, 0, 3, 4).reshape(b, h, nq * qb, d)


def banded_attention(q, k, v, max_dist):
    b, h, l, dh = q.shape
    blk = Q_BLOCK
    lp = -(-l // blk) * blk
    pad = ((0, 0), (0, 0), (0, lp - l), (0, 0))
    q, k, v = jnp.pad(q, pad), jnp.pad(k, pad), jnp.pad(v, pad)
    nb = lp // blk
    n_prev = -(-max_dist // blk)
    qb = q.reshape(b, h, nb, blk, dh)

    def context(a):
        ab = a.reshape(b, h, nb, blk, dh)
        ab = jnp.pad(ab, ((0, 0), (0, 0), (n_prev, 0), (0, 0), (0, 0)))
        return jnp.concatenate([ab[:, :, j:j + nb] for j in range(n_prev + 1)], axis=3)

    kc, vc = context(k), context(v)
    sc = jnp.einsum('bhnqd,bhnkd->bhnqk', qb, kc).astype(jnp.float32) * dh ** -0.5
    q_pos = jnp.arange(nb)[:, None, None] * blk + jnp.arange(blk)[None, :, None]
    k_pos = (jnp.arange(nb)[:, None, None] - n_prev) * blk + jnp.arange((n_prev + 1) * blk)[None, None, :]
    dist = q_pos - k_pos
    mask = (dist >= 0) & (dist <= max_dist) & (k_pos >= 0)
    sc = jnp.where(mask, sc, -jnp.inf)
    m = jnp.max(sc, -1, keepdims=True)
    p = jnp.exp(sc - m)
    den = jnp.sum(p, -1, keepdims=True)
    out = jnp.einsum('bhnqk,bhnkd->bhnqd', (p / den).astype(v.dtype), vc)
    lse = (m + jnp.log(den))[..., 0]
    return out.reshape(b, h, lp, dh)[:, :, :l], lse.reshape(b, h, lp)[:, :, :l]


def compress_tokens(t, pe, w1, w2):
    b, s, dh = t.shape
    n_c = (s - CMP_LEN) // CMP_STRIDE + 1
    idx = jnp.arange(n_c)[:, None] * CMP_STRIDE + jnp.arange(CMP_LEN)[None, :]
    blocks = t[:, idx] + pe
    hid = jax.nn.gelu(blocks.reshape(b, n_c, CMP_LEN * dh) @ w1)
    return hid @ w2


def nsa_attention(q, k_cmp, v_cmp, k_slc, v_slc, k_win, v_win, gate_logits, cmp_pe, cmp_w1, cmp_w2):
    b, h, s, dh = q.shape
    scale = dh ** -0.5
    t_pos = jnp.arange(s)
    kc = compress_tokens(k_cmp, cmp_pe[0], cmp_w1[0], cmp_w2[0])
    vc = compress_tokens(v_cmp, cmp_pe[1], cmp_w1[1], cmp_w2[1])
    n_c = kc.shape[1]
    sc = jnp.einsum('bhtd,bcd->bhtc', q, kc).astype(jnp.float32) * scale
    c_end = jnp.arange(n_c) * CMP_STRIDE + CMP_LEN - 1
    sc = jnp.where(c_end[None, :] <= t_pos[:, None], sc, -jnp.inf)
    m = jnp.max(sc, -1, keepdims=True)
    m = jnp.where(jnp.isfinite(m), m, 0.0)
    e = jnp.exp(sc - m)
    p_cmp = e / jnp.maximum(jnp.sum(e, -1, keepdims=True), TINY)
    o_cmp = jnp.einsum('bhtc,bcd->bhtd', p_cmp.astype(vc.dtype), vc)

    n_s = s // SLC_LEN
    k_top = min(SLC_TOPK, n_s)
    c_start = jnp.arange(n_c) * CMP_STRIDE
    j_start = jnp.arange(n_s) * SLC_LEN
    overlap = ((c_start[:, None] < j_start[None, :] + SLC_LEN)
               & (c_start[:, None] + CMP_LEN > j_start[None, :])).astype(jnp.float32)
    imp = jnp.einsum('bhtc,cj->btj', p_cmp, overlap)
    t_blk = t_pos // SLC_LEN
    j_idx = jnp.arange(n_s)
    forced = (j_idx[None] == 0) | (j_idx[None] == t_blk[:, None]) | (j_idx[None] == t_blk[:, None] - 1)
    imp = jnp.where(forced, imp + FORCE_BONUS, imp)
    imp = jnp.where(j_idx[None] <= t_blk[:, None], imp, -jnp.inf)
    _, sel = lax.top_k(imp, k_top)

    ks_blocks = k_slc.reshape(b, n_s, SLC_LEN, dh)
    vs_blocks = v_slc.reshape(b, n_s, SLC_LEN, dh)
    gather = jax.vmap(lambda blocks, ix: blocks[ix])
    nq = s // Q_BLOCK
    sel_c = sel.reshape(b, nq, Q_BLOCK, k_top).transpose(1, 0, 2, 3)

    def sel_block(args):
        qc, selc, bi = args
        kg = gather(ks_blocks, selc)
        vg = gather(vs_blocks, selc)
        scs = jnp.einsum('bhqd,bqnld->bhqnl', qc, kg).astype(jnp.float32) * scale
        q_pos = bi * Q_BLOCK + jnp.arange(Q_BLOCK)
        k_pos = selc[..., None] * SLC_LEN + jnp.arange(SLC_LEN)
        scs = jnp.where((k_pos <= q_pos[None, :, None, None])[:, None], scs, -jnp.inf)
        pr = jax.nn.softmax(scs.reshape(b, h, Q_BLOCK, k_top * SLC_LEN), axis=-1).reshape(scs.shape)
        return jnp.einsum('bhqnl,bqnld->bhqd', pr.astype(vg.dtype), vg)

    o_slc = from_query_blocks(lax.map(sel_block, (to_query_blocks(q), sel_c, jnp.arange(nq))))

    kw = jnp.broadcast_to(k_win[:, None], (b, h, s, dh))
    vw = jnp.broadcast_to(v_win[:, None], (b, h, s, dh))
    o_win, _ = banded_attention(q, kw, vw, WIN_LEN - 1)

    g = jax.nn.sigmoid(gate_logits.astype(jnp.float32)).reshape(b, s, 3, h).transpose(2, 0, 3, 1)[..., None]
    g = g.astype(q.dtype)
    return g[0] * o_cmp + g[1] * o_slc + g[2] * o_win


def dilated_attention(q, k, v):
    b, h, s, dh = q.shape
    outs, lses = [], []
    for window, dil in DILATED_PATTERNS:
        ls = s // dil

        def to_sub(a):
            return a.reshape(b, h, ls, dil, dh).transpose(0, 1, 3, 2, 4).reshape(b, h * dil, ls, dh)

        o, lse = banded_attention(to_sub(q), to_sub(k), to_sub(v), window // dil)
        outs.append(o.reshape(b, h, dil, ls, dh).transpose(0, 1, 3, 2, 4).reshape(b, h, s, dh))
        lses.append(lse.reshape(b, h, dil, ls).transpose(0, 1, 3, 2).reshape(b, h, s))
    w = jax.nn.softmax(jnp.stack(lses, 0), axis=0)
    return jnp.einsum('pbhs,pbhsd->bhsd', w.astype(q.dtype), jnp.stack(outs, 0))


def diff_attention(q, k, v, lam_params, norm_g, lam_init):
    b, h2, s, dq = q.shape
    h = h2 // 2
    lp = lam_params.astype(jnp.float32)
    lam = jnp.exp(jnp.sum(lp[0] * lp[1])) - jnp.exp(jnp.sum(lp[2] * lp[3])) + lam_init
    key_pos = jnp.arange(s)

    def block(args):
        qc, bi = args
        sc = jnp.einsum('bmqd,bmkd->bmqk', qc, k).astype(jnp.float32) * dq ** -0.5
        q_pos = bi * Q_BLOCK + jnp.arange(Q_BLOCK)
        sc = jnp.where(key_pos[None, :] <= q_pos[:, None], sc, -jnp.inf)
        a = jax.nn.softmax(sc, axis=-1).reshape(b, h, 2, Q_BLOCK, s)
        w = a[:, :, 0] - lam * a[:, :, 1]
        return jnp.einsum('bhqk,bhkd->bhqd', w.astype(v.dtype), v)

    o = from_query_blocks(lax.map(block, (to_query_blocks(q), jnp.arange(s // Q_BLOCK))))
    of = o.astype(jnp.float32)
    of = of * lax.rsqrt(jnp.mean(jnp.square(of), -1, keepdims=True) + RMS_EPS) * norm_g.astype(jnp.float32)
    return (of * (1.0 - lam_init)).astype(v.dtype)


def stick_breaking_attention(q, k, v):
    b, h, s, dh = q.shape
    key_pos = jnp.arange(s)

    def block(args):
        qc, bi = args
        z = jnp.einsum('bhqd,bhkd->bhqk', qc, k).astype(jnp.float32) * dh ** -0.5
        q_pos = bi * Q_BLOCK + jnp.arange(Q_BLOCK)
        strict = key_pos[None, :] < q_pos[:, None]
        log_beta = jax.nn.log_sigmoid(z)
        log_1m = jnp.where(strict, jax.nn.log_sigmoid(-z), 0.0)
        tail = lax.cumsum(log_1m, axis=3, reverse=True) - log_1m
        a = jnp.where(strict, jnp.exp(log_beta + tail), 0.0)
        return jnp.einsum('bhqk,bhkd->bhqd', a.astype(v.dtype), v)

    return from_query_blocks(lax.map(block, (to_query_blocks(q), jnp.arange(s // Q_BLOCK))))


def mixing_sublayer(x, w_in, w_out, cmp_pe, cmp_w1, cmp_w2, lam_params, diff_g, lam_init):
    proj = jnp.einsum('bsd,dc->bsc', x, w_in)
    (a_q, a_kc, a_vc, a_ks, a_vs, a_kw, a_vw, a_g,
     b_q, b_k, b_v, c_q, c_k, c_v, d_q, d_k, d_v) = split_cols(proj, IN_WIDTHS)
    o_a = nsa_attention(partial_rope(split_heads(a_q, N_HEADS_A)), rope_single(a_kc), a_vc,
                        rope_single(a_ks), a_vs, rope_single(a_kw), a_vw, a_g, cmp_pe, cmp_w1, cmp_w2)
    o_b = dilated_attention(partial_rope(split_heads(b_q, N_HEADS_B)), partial_rope(split_heads(b_k, N_HEADS_B)),
                            split_heads(b_v, N_HEADS_B))
    o_c = diff_attention(partial_rope(split_heads(c_q, 2 * N_HEADS_C)), partial_rope(split_heads(c_k, 2 * N_HEADS_C)),
                         split_heads(c_v, N_HEADS_C), lam_params, diff_g, lam_init)
    o_d = stick_breaking_attention(split_heads(d_q, N_HEADS_D), split_heads(d_k, N_HEADS_D),
                                   split_heads(d_v, N_HEADS_D))
    o = jnp.concatenate([o_a, o_b, o_c, o_d], axis=1)
    b, h, s, dh = o.shape
    return o.transpose(0, 2, 1, 3).reshape(b, s, h * dh) @ w_out


def peer_ffn(x, w_q, sub_keys, u_tab, v_tab):
    b, s, d = x.shape
    t = b * s
    xt = x.reshape(t, d)
    q = (xt @ w_q).reshape(t, PEER_HEADS, 2, PEER_QDIM // 2)
    sc = jnp.einsum('thpc,hpnc->thpn', q, sub_keys).astype(jnp.float32)
    s1, i1 = lax.top_k(sc[:, :, 0], PEER_TOPK)
    s2, i2 = lax.top_k(sc[:, :, 1], PEER_TOPK)
    cand = (s1[..., :, None] + s2[..., None, :]).reshape(t, PEER_HEADS, PEER_TOPK * PEER_TOPK)
    cidx = (i1[..., :, None] * PEER_NKEYS + i2[..., None, :]).reshape(t, PEER_HEADS, PEER_TOPK * PEER_TOPK)
    top_s, pos = lax.top_k(cand, PEER_TOPK)
    eidx = jnp.take_along_axis(cidx, pos, axis=-1)
    gate = jax.nn.softmax(top_s, axis=-1).astype(x.dtype)
    nch = t // PEER_CHUNK

    def chunk(args):
        xc, ec, gc = args
        u = u_tab[ec]
        act = jax.nn.gelu(jnp.einsum('cd,chkd->chk', xc, u))
        vv = v_tab[ec]
        return jnp.einsum('chk,chkd->cd', gc * act, vv)

    out = lax.map(chunk, (xt.reshape(nch, PEER_CHUNK, d),
                          eidx.reshape(nch, PEER_CHUNK, PEER_HEADS, PEER_TOPK),
                          gate.reshape(nch, PEER_CHUNK, PEER_HEADS, PEER_TOPK)))
    return out.reshape(b, s, d)


def setup_inputs(seed: int = 0) -> dict:
    key = jax.random.key(seed)
    ks = jax.random.split(key, 16)

    def nrm(k, shape, scale):
        return jax.random.normal(k, shape, jnp.float32) * scale

    return {
        'x': nrm(ks[0], (BATCH, SEQ, D_MODEL), 1.0),
        'w_in': nrm(ks[1], (DEPTH, D_MODEL, IN_WIDTH), D_MODEL ** -0.5),
        'w_out': nrm(ks[2], (DEPTH, MIX_WIDTH, D_MODEL), DEEPNORM_BETA * MIX_WIDTH ** -0.5),
        'nsa_cmp_pe': nrm(ks[3], (DEPTH, 2, CMP_LEN, HEAD_DIM), 0.02),
        'nsa_cmp_w1': nrm(ks[4], (DEPTH, 2, CMP_LEN * HEAD_DIM, CMP_HIDDEN), (CMP_LEN * HEAD_DIM) ** -0.5),
        'nsa_cmp_w2': nrm(ks[5], (DEPTH, 2, CMP_HIDDEN, HEAD_DIM), CMP_HIDDEN ** -0.5),
        'diff_lambda': nrm(ks[6], (DEPTH, 4, DIFF_QK_DIM), 0.1),
        'diff_norm_g': 1.0 + nrm(ks[7], (DEPTH, HEAD_DIM), 0.02),
        'ln1_g': 1.0 + nrm(ks[8], (DEPTH, D_MODEL), 0.02),
        'ln1_b': nrm(ks[9], (DEPTH, D_MODEL), 0.02),
        'peer_wq': nrm(ks[10], (DEPTH, D_MODEL, PEER_HEADS * PEER_QDIM), D_MODEL ** -0.5),
        'peer_subkeys': nrm(ks[11], (DEPTH, PEER_HEADS, 2, PEER_NKEYS, PEER_QDIM // 2), (PEER_QDIM // 2) ** -0.5),
        'peer_u': nrm(ks[12], (DEPTH, PEER_EXPERTS, D_MODEL), D_MODEL ** -0.5),
        'peer_v': nrm(ks[13], (DEPTH, PEER_EXPERTS, D_MODEL), DEEPNORM_BETA * PEER_HEADS ** -0.5),
        'ln2_g': 1.0 + nrm(ks[14], (DEPTH, D_MODEL), 0.02),
        'ln2_b': nrm(ks[15], (DEPTH, D_MODEL), 0.02),
    }


def reference(x, w_in, w_out, nsa_cmp_pe, nsa_cmp_w1, nsa_cmp_w2, diff_lambda, diff_norm_g,
              ln1_g, ln1_b, peer_wq, peer_subkeys, peer_u, peer_v, ln2_g, ln2_b):
    for l in range(DEPTH):
        lam_init = 0.8 - 0.6 * math.exp(-0.3 * l)
        mix = mixing_sublayer(x, w_in[l], w_out[l], nsa_cmp_pe[l], nsa_cmp_w1[l], nsa_cmp_w2[l],
                              diff_lambda[l], diff_norm_g[l], lam_init)
        x = layer_norm(DEEPNORM_ALPHA * x + mix, ln1_g[l], ln1_b[l])
        ffn = peer_ffn(x, peer_wq[l], peer_subkeys[l], peer_u[l], peer_v[l])
        x = layer_norm(DEEPNORM_ALPHA * x + ffn, ln2_g[l], ln2_b[l])
    return x
```

```python
import functools
import math

import numpy as np
import jax
import jax.numpy as jnp
from jax import lax
from jax.experimental import pallas as pl
from jax.experimental.pallas import tpu as pltpu

F32 = jnp.float32
BF16 = jnp.bfloat16

D_MODEL = 2048
DEPTH = 4
HEAD_DIM = 128
ROPE_THETA = 500000.0
CMP_LEN = 32
CMP_STRIDE = 16
CMP_HIDDEN = 256
SLC_LEN = 64
SLC_TOPK = 16
WIN_LEN = 512
FORCE_BONUS = 1000.0
DILATED_PATTERNS = ((128, 1), (512, 4), (2048, 16))
DIFF_QK_DIM = 64
PEER_HEADS = 8
PEER_NKEYS = 128
PEER_TOPK = 16
LN_EPS = 1e-5
RMS_EPS = 1e-6
TINY = 1e-30
DEEPNORM_ALPHA = (2 * DEPTH) ** 0.25
NEG = -1e30

LANES = 128
VMEM_LIMIT_BYTES = 56 * 1024 * 1024

G_AQ, G_BQ, G_BK, G_AKC, G_AKS, G_AKW = 0, 4, 8, 12, 13, 14
G_CQ, G_CK = 16, 20
G_AVC, G_AVS, G_AVW, G_AG, G_BV, G_CV, G_DQ, G_DK, G_DV = 24, 25, 26, 27, 28, 32, 36, 40, 44
N_GROUPS = 48
PROJ_TN = 512
N_R128_TILES = 16 * LANES // PROJ_TN
N_R64_TILES = 8 * LANES // PROJ_TN
_IN_SEGMENTS = (
    (0, 512, G_AQ), (512, 128, G_AKC), (640, 128, G_AVC), (768, 128, G_AKS), (896, 128, G_AVS),
    (1024, 128, G_AKW), (1152, 128, G_AVW), (1280, 12, G_AG),
    (1292, 512, G_BQ), (1804, 512, G_BK), (2316, 512, G_BV),
    (2828, 512, G_CQ), (3340, 512, G_CK), (3852, 512, G_CV),
    (4364, 512, G_DQ), (4876, 512, G_DK), (5388, 512, G_DV),
)


def _params(n_axes):
    return pltpu.CompilerParams(dimension_semantics=("arbitrary",) * n_axes,
                                vmem_limit_bytes=VMEM_LIMIT_BYTES)


def _dot(a, b):
    return jnp.dot(a, b, preferred_element_type=F32)


def _dot_nt(a, b):
    return lax.dot_general(a, b, (((1,), (1,)), ((), ())), preferred_element_type=F32)


def _split_bf16(x):
    hi = x.astype(BF16)
    lo = (x - hi.astype(F32)).astype(BF16)
    return hi, lo


def _rope_groups(acc, c, s_lo, s_hi, half):
    outs = []
    for g in range(acc.shape[1] // LANES):
        xg = acc[:, g * LANES:(g + 1) * LANES]
        outs.append(xg * c + pltpu.roll(xg, LANES - half, 1) * s_lo + pltpu.roll(xg, half, 1) * s_hi)
    return jnp.concatenate(outs, axis=1)


def _proj_kernel(x_ref, w_ref, c128, lo128, hi128, c64, lo64, hi64, o_ref):
    j = pl.program_id(1)
    acc = _dot(x_ref[...], w_ref[...])

    @pl.when(j < N_R128_TILES)
    def _():
        o_ref[...] = _rope_groups(acc, c128[...], lo128[...], hi128[...], 16).astype(BF16)

    @pl.when((j >= N_R128_TILES) & (j < N_R128_TILES + N_R64_TILES))
    def _():
        o_ref[...] = _rope_groups(acc, c64[...], lo64[...], hi64[...], 8).astype(BF16)

    @pl.when(j >= N_R128_TILES + N_R64_TILES)
    def _():
        o_ref[...] = acc.astype(BF16)


def _project(x_bf, w_p, tables, seq):
    t, d = x_bf.shape
    n = w_p.shape[1]
    tm = min(1024, seq)
    n_pos = seq // tm
    tab_spec = pl.BlockSpec((tm, LANES), lambda i, j: (i % n_pos, 0))
    return pl.pallas_call(
        _proj_kernel,
        grid=(t // tm, n // PROJ_TN),
        in_specs=[pl.BlockSpec((tm, d), lambda i, j: (i, 0)),
                  pl.BlockSpec((d, PROJ_TN), lambda i, j: (0, j))] + [tab_spec] * 6,
        out_specs=pl.BlockSpec((tm, PROJ_TN), lambda i, j: (i, j)),
        out_shape=jax.ShapeDtypeStruct((t, n), BF16),
        compiler_params=_params(2),
        name="in_proj_rope",
    )(x_bf, w_p, *tables)


def _rope_tables(seq):
    tabs = []
    pos = jnp.arange(seq, dtype=F32)[:, None]
    for dh in (HEAD_DIM, DIFF_QK_DIM):
        rot = dh // 4
        half = rot // 2
        inv_freq = ROPE_THETA ** (-jnp.arange(half, dtype=F32) / half)
        ang = pos * inv_freq[None, :]
        cos, sin = jnp.cos(ang), jnp.sin(ang)
        ones = jnp.ones((seq, dh - rot), F32)
        zeros_h = jnp.zeros((seq, half), F32)
        zeros_r = jnp.zeros((seq, dh - rot), F32)
        c = jnp.concatenate([cos, cos, ones], axis=1)
        lo = jnp.concatenate([-sin, zeros_h, zeros_r], axis=1)
        hi = jnp.concatenate([zeros_h, sin, zeros_r], axis=1)
        reps = LANES // dh
        tabs += [jnp.tile(c, (1, reps)), jnp.tile(lo, (1, reps)), jnp.tile(hi, (1, reps))]
    return tabs


def _prep_w_in(w_in):
    nl, d, _ = w_in.shape
    cols = [None] * N_GROUPS
    for off, width, g in _IN_SEGMENTS:
        if width % LANES == 0:
            for k in range(width // LANES):
                cols[g + k] = w_in[:, :, off + k * LANES: off + (k + 1) * LANES]
        else:
            cols[g] = jnp.pad(w_in[:, :, off:off + width], ((0, 0), (0, 0), (0, LANES - width)))
    zero = jnp.zeros((nl, d, LANES), w_in.dtype)
    cols = [zero if c is None else c for c in cols]
    return jnp.concatenate(cols, axis=2).astype(BF16)


def _compress_kernel(ch_ref, w1_ref, pe_ref, w2_ref, o_ref):
    ch = ch_ref[0, 0]
    n_chunks, half_in = ch.shape
    first = _dot(ch, w1_ref[0, :half_in, :])
    second = _dot(ch, w1_ref[0, half_in:, :])
    pe_term = _dot(pe_ref[0], w1_ref[0])[0:1]
    hid = first + pltpu.roll(second, n_chunks - 1, 0) + pe_term
    hid = jax.nn.gelu(hid)
    o_ref[0, 0] = _dot(hid.astype(BF16), w2_ref[0]).astype(BF16)


def _compress(chunks, w1, pe, w2):
    b, _, n_chunks, width = chunks.shape
    return pl.pallas_call(
        _compress_kernel,
        grid=(b, 2),
        in_specs=[pl.BlockSpec((1, 1, n_chunks, width), lambda i, j: (i, j, 0, 0)),
                  pl.BlockSpec((1, 2 * width, CMP_HIDDEN), lambda i, j: (j, 0, 0)),
                  pl.BlockSpec((1, 8, 2 * width), lambda i, j: (j, 0, 0)),
                  pl.BlockSpec((1, CMP_HIDDEN, HEAD_DIM), lambda i, j: (j, 0, 0))],
        out_specs=pl.BlockSpec((1, 1, n_chunks, HEAD_DIM), lambda i, j: (i, j, 0, 0)),
        out_shape=jax.ShapeDtypeStruct((b, 2, n_chunks, HEAD_DIM), BF16),
        compiler_params=_params(2),
        name="nsa_compress",
    )(chunks, w1, pe, w2)


def _online_update(s, valid, vblk, m_scr, l_scr, acc_scr):
    s = jnp.where(valid, s, NEG)
    m_old = m_scr[...]
    m_new = jnp.maximum(m_old, jnp.max(s, axis=1, keepdims=True))
    p = jnp.where(valid, jnp.exp(s - m_new), 0.0)
    alpha = jnp.exp(m_old - m_new)
    l_scr[...] = alpha * l_scr[...] + jnp.sum(p, axis=1, keepdims=True)
    acc_scr[...] = alpha * acc_scr[...] + _dot(p.astype(BF16), vblk)
    m_scr[...] = m_new


def _reset(m_scr, l_scr, acc_scr):
    m_scr[...] = jnp.full(m_scr.shape, NEG, F32)
    l_scr[...] = jnp.zeros(l_scr.shape, F32)
    acc_scr[...] = jnp.zeros(acc_scr.shape, F32)


NSA_TQ = 128
NSA_KB = 256
NSA_WB = 128
NSA_HEADS = 4


def _nsa_kernel(q_ref, kc_ref, vc_ref, ks_ref, vs_ref, kw_ref, vw_ref, g_ref, ovl_ref, exp_ref,
                o_ref, m_scr, l_scr, acc_scr):
    i = pl.program_id(1)
    q0 = i * NSA_TQ
    scale = HEAD_DIM ** -0.5
    rows = NSA_HEADS * NSA_TQ
    q = q_ref[0]
    q4 = jnp.concatenate([q[:, h * LANES:(h + 1) * LANES] for h in range(NSA_HEADS)], axis=0)

    kc = kc_ref[0, 0]
    n_c = kc.shape[0]
    s = _dot_nt(q4, kc) * scale
    tpos4 = q0 + (lax.broadcasted_iota(jnp.int32, (rows, n_c), 0) & (NSA_TQ - 1))
    c_end = lax.broadcasted_iota(jnp.int32, (rows, n_c), 1) * CMP_STRIDE + (CMP_LEN - 1)
    vis = c_end <= tpos4
    s = jnp.where(vis, s, NEG)
    m = jnp.max(s, axis=1, keepdims=True)
    e = jnp.where(vis, jnp.exp(s - m), 0.0)
    p_cmp = e / jnp.maximum(jnp.sum(e, axis=1, keepdims=True), TINY)
    o_cmp = _dot(p_cmp.astype(BF16), vc_ref[0, 0])

    p_sum = p_cmp[0:NSA_TQ]
    for h in range(1, NSA_HEADS):
        p_sum = p_sum + p_cmp[h * NSA_TQ:(h + 1) * NSA_TQ]
    p_hi, p_lo = _split_bf16(p_sum)
    imp = _dot(p_hi, ovl_ref[...]) + _dot(p_lo, ovl_ref[...])
    j_idx = lax.broadcasted_iota(jnp.int32, (NSA_TQ, LANES), 1)
    t_blk = (q0 + lax.broadcasted_iota(jnp.int32, (NSA_TQ, LANES), 0)) // SLC_LEN
    forced = (j_idx == 0) | (j_idx == t_blk) | (j_idx == t_blk - 1)
    imp = jnp.where(forced, imp + FORCE_BONUS, imp)
    allowed = j_idx <= t_blk
    imp = jnp.where(allowed, imp, -jnp.inf)
    n_s = ks_ref.shape[1] // SLC_LEN
    rank = jnp.zeros((NSA_TQ, LANES), F32)
    for c in range(n_s):
        col = imp[:, c:c + 1]
        gt = jnp.where(col > imp, 1.0, 0.0)
        ge = jnp.where(col >= imp, 1.0, 0.0)
        rank = rank + jnp.where(j_idx > c, ge, gt)
    sel = jnp.where(allowed, jnp.where(rank < float(SLC_TOPK), 1.0, 0.0), 0.0).astype(BF16)

    _reset(m_scr, l_scr, acc_scr)
    row_t = q0 + lax.broadcasted_iota(jnp.int32, (NSA_TQ, NSA_KB), 0)
    col_k = lax.broadcasted_iota(jnp.int32, (NSA_TQ, NSA_KB), 1)

    def slc_body(jb, carry):
        k0 = pl.multiple_of(jb * NSA_KB, NSA_KB)
        kblk = ks_ref[0, pl.ds(k0, NSA_KB), :]
        vblk = vs_ref[0, pl.ds(k0, NSA_KB), :]
        sc = _dot_nt(q4, kblk) * scale
        picked = _dot(sel, exp_ref[jb])
        vm = jnp.where(k0 + col_k <= row_t, picked, 0.0)
        valid = jnp.concatenate([vm] * NSA_HEADS, axis=0) > 0.5
        _online_update(sc, valid, vblk, m_scr, l_scr, acc_scr)
        return carry

    lax.fori_loop(0, (q0 + NSA_TQ - 1) // NSA_KB + 1, slc_body, 0)
    o_slc = acc_scr[...] / l_scr[...]

    _reset(m_scr, l_scr, acc_scr)
    row_w = q0 + lax.broadcasted_iota(jnp.int32, (NSA_TQ, NSA_WB), 0)
    col_w = lax.broadcasted_iota(jnp.int32, (NSA_TQ, NSA_WB), 1)

    def win_body(jb, carry):
        k0 = pl.multiple_of(jb * NSA_WB, NSA_WB)
        kblk = kw_ref[0, pl.ds(k0, NSA_WB), :]
        vblk = vw_ref[0, pl.ds(k0, NSA_WB), :]
        sc = _dot_nt(q4, kblk) * scale
        dist = row_w - (k0 + col_w)
        vm = jnp.where((dist >= 0) & (dist <= WIN_LEN - 1), 1.0, 0.0)
        valid = jnp.concatenate([vm] * NSA_HEADS, axis=0) > 0.5
        _online_update(sc, valid, vblk, m_scr, l_scr, acc_scr)
        return carry

    first_w = jnp.maximum(q0 - WIN_LEN, 0) // NSA_WB
    lax.fori_loop(first_w, i * (NSA_TQ // NSA_WB) + NSA_TQ // NSA_WB, win_body, 0)
    o_win = acc_scr[...] / l_scr[...]

    gate = jax.nn.sigmoid(g_ref[0].astype(F32))
    for h in range(NSA_HEADS):
        rs = slice(h * NSA_TQ, (h + 1) * NSA_TQ)
        o_h = (gate[:, h:h + 1] * o_cmp[rs] + gate[:, NSA_HEADS + h:NSA_HEADS + h + 1] * o_slc[rs]
               + gate[:, 2 * NSA_HEADS + h:2 * NSA_HEADS + h + 1] * o_win[rs])
        o_ref[0, :, h * LANES:(h + 1) * LANES] = o_h.astype(BF16)


def _nsa_consts(seq):
    n_chunks = seq // CMP_STRIDE
    n_s = seq // SLC_LEN
    c_start = np.arange(n_chunks)[:, None] * CMP_STRIDE
    j_start = np.arange(LANES)[None, :] * SLC_LEN
    real = (np.arange(n_chunks)[:, None] < n_chunks - 1) & (np.arange(LANES)[None, :] < n_s)
    ovl = ((c_start < j_start + SLC_LEN) & (c_start + CMP_LEN > j_start) & real).astype(np.float32)
    key_blk = (np.arange(seq) // SLC_LEN).reshape(seq // NSA_KB, 1, NSA_KB)
    expand = (key_blk == np.arange(LANES)[None, :, None]).astype(np.float32)
    return jnp.asarray(ovl, BF16), jnp.asarray(expand, BF16)


def _nsa_attention(p3, kvc, consts):
    b, seq, _ = p3.shape
    ovl, expand = consts
    n_chunks = kvc.shape[2]
    full = lambda g: pl.BlockSpec((1, seq, LANES), lambda bi, i, g=g: (bi, 0, g))
    return pl.pallas_call(
        _nsa_kernel,
        grid=(b, seq // NSA_TQ),
        in_specs=[pl.BlockSpec((1, NSA_TQ, NSA_HEADS * LANES), lambda bi, i: (bi, i, G_AQ // NSA_HEADS)),
                  pl.BlockSpec((1, 1, n_chunks, LANES), lambda bi, i: (bi, 0, 0, 0)),
                  pl.BlockSpec((1, 1, n_chunks, LANES), lambda bi, i: (bi, 1, 0, 0)),
                  full(G_AKS), full(G_AVS), full(G_AKW), full(G_AVW),
                  pl.BlockSpec((1, NSA_TQ, LANES), lambda bi, i: (bi, i, G_AG)),
                  pl.BlockSpec(ovl.shape, lambda bi, i: (0, 0)),
                  pl.BlockSpec(expand.shape, lambda bi, i: (0, 0, 0))],
        out_specs=pl.BlockSpec((1, NSA_TQ, NSA_HEADS * LANES), lambda bi, i: (bi, i, 0)),
        out_shape=jax.ShapeDtypeStruct((b, seq, NSA_HEADS * LANES), BF16),
        scratch_shapes=[pltpu.VMEM((NSA_HEADS * NSA_TQ, 1), F32), pltpu.VMEM((NSA_HEADS * NSA_TQ, 1), F32),
                        pltpu.VMEM((NSA_HEADS * NSA_TQ, LANES), F32)],
        compiler_params=_params(2),
        name="nsa_attention",
    )(p3, kvc, kvc, p3, p3, p3, p3, p3, ovl, expand)


DIL_TQ = 256
DIL_KB = 256
DIL_SPAN = max(w for w, _ in DILATED_PATTERNS)


def _dil_kernel(q_ref, k_ref, v_ref, cnt_ref, add_ref, o_ref, m_scr, l_scr, acc_scr):
    i = pl.program_id(2)
    scale = HEAD_DIM ** -0.5
    q = q_ref[0]
    _reset(m_scr, l_scr, acc_scr)

    def body(jb, carry):
        k0 = pl.multiple_of(jb * DIL_KB, DIL_KB)
        kblk = k_ref[0, pl.ds(k0, DIL_KB), :]
        vblk = v_ref[0, pl.ds(k0, DIL_KB), :]
        s = _dot_nt(q, kblk) * scale + add_ref[i - jb]
        m_old = m_scr[...]
        m_new = jnp.maximum(m_old, jnp.max(s, axis=1, keepdims=True))
        p = jnp.exp(s - m_new) * cnt_ref[i - jb]
        alpha = jnp.exp(m_old - m_new)
        l_scr[...] = alpha * l_scr[...] + jnp.sum(p, axis=1, keepdims=True)
        acc_scr[...] = alpha * acc_scr[...] + _dot(p.astype(BF16), vblk)
        m_scr[...] = m_new
        return carry

    lax.fori_loop(jnp.maximum(i - DIL_SPAN // DIL_KB, 0), i + 1, body, 0)
    o_ref[0] = (acc_scr[...] / l_scr[...]).astype(BF16)


def _dil_consts():
    n_off = DIL_SPAN // DIL_KB + 1
    d = (np.arange(n_off)[:, None, None] * DIL_KB + np.arange(DIL_TQ)[None, :, None]
         - np.arange(DIL_KB)[None, None, :])
    cnt = np.zeros(d.shape, np.float32)
    for window, dil in DILATED_PATTERNS:
        cnt += ((d >= 0) & (d <= window) & (d % dil == 0)).astype(np.float32)
    add = np.where(cnt > 0, 0.0, NEG).astype(np.float32)
    return jnp.asarray(cnt), jnp.asarray(add)


def _head_attention(kernel, name, p3, gq, gk, gv, tq, consts, extra_scratch=()):
    b, seq, _ = p3.shape
    n_heads = 4
    const_specs = [pl.BlockSpec(c.shape, lambda bi, h, i, nd=c.ndim: (0,) * nd) for c in consts]
    return pl.pallas_call(
        kernel,
        grid=(b, n_heads, seq // tq),
        in_specs=[pl.BlockSpec((1, tq, LANES), lambda bi, h, i: (bi, i, gq + h)),
                  pl.BlockSpec((1, seq, LANES), lambda bi, h, i: (bi, 0, gk + h)),
                  pl.BlockSpec((1, seq, LANES), lambda bi, h, i: (bi, 0, gv + h))] + const_specs,
        out_specs=pl.BlockSpec((1, tq, LANES), lambda bi, h, i: (bi, i, h)),
        out_shape=jax.ShapeDtypeStruct((b, seq, n_heads * LANES), BF16),
        scratch_shapes=list(extra_scratch),
        compiler_params=_params(3),
        name=name,
    )(p3, p3, p3, *consts)


def _softmax_scratch(rows):
    return [pltpu.VMEM((rows, 1), F32), pltpu.VMEM((rows, 1), F32), pltpu.VMEM((rows, LANES), F32)]


DIFF_TQ = 256
DIFF_KB = 256


def _diff_kernel(q_ref, k_ref, v_ref, lam_ref, g_ref, init_ref, o_ref, m_scr, l_scr, acc_scr):
    i = pl.program_id(2)
    q0 = i * DIFF_TQ
    scale = DIFF_QK_DIM ** -0.5
    q = q_ref[0]
    lane = lax.broadcasted_iota(jnp.int32, q.shape, 1)
    zero = jnp.zeros_like(q)
    q2 = jnp.concatenate([jnp.where(lane < DIFF_QK_DIM, q, zero), jnp.where(lane >= DIFF_QK_DIM, q, zero)],
                         axis=0)
    _reset(m_scr, l_scr, acc_scr)
    row_t = q0 + (lax.broadcasted_iota(jnp.int32, (2 * DIFF_TQ, DIFF_KB), 0) & (DIFF_TQ - 1))
    col_k = lax.broadcasted_iota(jnp.int32, (2 * DIFF_TQ, DIFF_KB), 1)

    def body(jb, carry):
        k0 = pl.multiple_of(jb * DIFF_KB, DIFF_KB)
        kblk = k_ref[0, pl.ds(k0, DIFF_KB), :]
        vblk = v_ref[0, pl.ds(k0, DIFF_KB), :]
        s = _dot_nt(q2, kblk) * scale
        _online_update(s, k0 + col_k <= row_t, vblk, m_scr, l_scr, acc_scr)
        return carry

    lax.fori_loop(0, (q0 + DIFF_TQ - 1) // DIFF_KB + 1, body, 0)
    a = acc_scr[...] / l_scr[...]
    lp = lam_ref[...]
    lam_init = init_ref[...]
    lam = (jnp.exp(jnp.sum(lp[0:1] * lp[1:2], axis=1, keepdims=True))
           - jnp.exp(jnp.sum(lp[2:3] * lp[3:4], axis=1, keepdims=True)) + lam_init)
    o = a[:DIFF_TQ] - lam * a[DIFF_TQ:]
    o = o * lax.rsqrt(jnp.mean(jnp.square(o), axis=1, keepdims=True) + RMS_EPS) * g_ref[...]
    o_ref[0] = (o * (1.0 - lam_init)).astype(BF16)


SB_TQ = 256
SB_KB = 256


def _sb_kernel(q_ref, k_ref, v_ref, tri_ref, o_ref, run_scr, acc_scr):
    i = pl.program_id(2)
    q0 = i * SB_TQ
    scale = HEAD_DIM ** -0.5
    q = q_ref[0]
    run_scr[...] = jnp.zeros(run_scr.shape, F32)
    acc_scr[...] = jnp.zeros(acc_scr.shape, F32)
    row_t = q0 + lax.broadcasted_iota(jnp.int32, (SB_TQ, SB_KB), 0)
    col_k = lax.broadcasted_iota(jnp.int32, (SB_TQ, SB_KB), 1)
    n_blocks = (q0 + SB_TQ - 1) // SB_KB + 1

    def body(step, carry):
        jb = n_blocks - 1 - step
        k0 = pl.multiple_of(jb * SB_KB, SB_KB)
        kblk = k_ref[0, pl.ds(k0, SB_KB), :]
        vblk = v_ref[0, pl.ds(k0, SB_KB), :]
        z = _dot_nt(q, kblk) * scale
        strict = k0 + col_k < row_t
        softplus = jnp.maximum(z, 0.0) + jnp.log(1.0 + jnp.exp(-jnp.abs(z)))
        log_1m = jnp.where(strict, -softplus, 0.0)
        hi, lo = _split_bf16(log_1m)
        tail = _dot(hi, tri_ref[...]) + _dot(lo, tri_ref[...])
        run = run_scr[...]
        a = jnp.where(strict, jnp.exp(z - softplus + tail + run), 0.0)
        acc_scr[...] += _dot(a.astype(BF16), vblk)
        run_scr[...] = run + tail[:, 0:1] + log_1m[:, 0:1]
        return carry

    lax.fori_loop(0, n_blocks, body, 0)
    o_ref[0] = acc_scr[...].astype(BF16)


def _sb_consts():
    j = np.arange(SB_KB)
    return (jnp.asarray((j[:, None] > j[None, :]).astype(np.float32), BF16),)


def _layer_norm(y, g, b):
    mu = jnp.mean(y, axis=1, keepdims=True)
    yc = y - mu
    var = jnp.mean(jnp.square(yc), axis=1, keepdims=True)
    return yc * lax.rsqrt(var + LN_EPS) * g + b


def _outproj_kernel(oa, ob, oc, od, w_ref, x_ref, g_ref, b_ref, y_ref, ybf_ref):
    width = oa.shape[1]
    mix = _dot(oa[...], w_ref[0:width, :])
    for k, o in enumerate((ob, oc, od), start=1):
        mix = mix + _dot(o[...], w_ref[k * width:(k + 1) * width, :])
    y = _layer_norm(DEEPNORM_ALPHA * x_ref[...] + mix, g_ref[...], b_ref[...])
    y_ref[...] = y
    ybf_ref[...] = y.astype(BF16)


def _outproj_ln(o_parts, w_out, x, g, b):
    t, d = x.shape
    tm = 512
    width = o_parts[0].shape[1]
    row = lambda w: pl.BlockSpec((tm, w), lambda i: (i, 0))
    vec = pl.BlockSpec((1, d), lambda i: (0, 0))
    return pl.pallas_call(
        _outproj_kernel,
        grid=(t // tm,),
        in_specs=[row(width)] * 4 + [pl.BlockSpec(w_out.shape, lambda i: (0, 0)), row(d), vec, vec],
        out_specs=[row(d), row(d)],
        out_shape=[jax.ShapeDtypeStruct((t, d), F32), jax.ShapeDtypeStruct((t, d), BF16)],
        compiler_params=_params(1),
        name="out_proj_ln",
    )(*o_parts, w_out, x, g, b)


PEER_TT = 512


def _top_values(x, n):
    out = []
    for r in range(n):
        mx = jnp.max(x, axis=0, keepdims=True)
        out.append(mx)
        if r + 1 < n:
            x = jnp.where(x == mx, -jnp.inf, x)
    return out


def _peer_a_kernel(xt_ref, wq_ref, sk_ref, a_ref, b_ref, ea_ref, eb_ref, tau_ref, q_scr, cand_scr):
    q_scr[...] = _dot(wq_ref[...], xt_ref[...]).astype(BF16)
    n_pairs = cand_scr.shape[0]

    def head(h, carry):
        r0 = pl.multiple_of(h * 2 * PEER_NKEYS, 2 * PEER_NKEYS)
        a = _dot(sk_ref[2 * h], q_scr[pl.ds(r0, PEER_NKEYS), :])
        b = _dot(sk_ref[2 * h + 1], q_scr[pl.ds(r0 + PEER_NKEYS, PEER_NKEYS), :])
        top_a = _top_values(a, PEER_TOPK)
        top_b = _top_values(b, PEER_TOPK)
        r = 0
        for ia in range(PEER_TOPK):
            for ib in range(PEER_TOPK // (ia + 1)):
                cand_scr[r:r + 1, :] = top_a[ia] + top_b[ib]
                r += 1
        cand_scr[r:n_pairs, :] = jnp.full((n_pairs - r, cand_scr.shape[1]), -jnp.inf, F32)
        top_s = _top_values(cand_scr[...], PEER_TOPK)
        z = jnp.ones_like(top_s[0])
        for t in top_s[1:]:
            z = z + jnp.exp(t - top_s[0])
        a_ref[h] = a
        b_ref[h] = b
        ea_ref[h] = jnp.exp(a - top_a[0]) / z
        eb_ref[h] = jnp.exp(b - top_b[0])
        tau_ref[h] = jnp.broadcast_to(top_s[-1], tau_ref.shape[1:])
        return carry

    lax.fori_loop(0, PEER_HEADS, head, 0)


def _peer_a(x_t, wq_t, sub_keys):
    d, t = x_t.shape
    tt = PEER_TT
    big = pl.BlockSpec((PEER_HEADS, PEER_NKEYS, tt), lambda i: (0, 0, i))
    big_shape = jax.ShapeDtypeStruct((PEER_HEADS, PEER_NKEYS, t), F32)
    n_pairs = 56
    return pl.pallas_call(
        _peer_a_kernel,
        grid=(t // tt,),
        in_specs=[pl.BlockSpec((d, tt), lambda i: (0, i)),
                  pl.BlockSpec(wq_t.shape, lambda i: (0, 0)),
                  pl.BlockSpec(sub_keys.shape, lambda i: (0, 0, 0))],
        out_specs=[big, big, big, big, pl.BlockSpec((PEER_HEADS, 8, tt), lambda i: (0, 0, i))],
        out_shape=[big_shape] * 4 + [jax.ShapeDtypeStruct((PEER_HEADS, 8, t), F32)],
        scratch_shapes=[pltpu.VMEM((wq_t.shape[0], tt), BF16), pltpu.VMEM((n_pairs, tt), F32)],
        compiler_params=_params(1),
        name="peer_scores_topk",
    )(x_t, wq_t, sub_keys)


PEER_ET = 512


def _peer_b_kernel(xt_ref, u_ref, vt_ref, a_ref, b_ref, ea_ref, eb_ref, tau_ref, o_ref, acc_scr):
    e = pl.program_id(1)

    @pl.when(e == 0)
    def _():
        acc_scr[...] = jnp.zeros(acc_scr.shape, F32)

    act = jax.nn.gelu(_dot(u_ref[...], xt_ref[...]))
    parts = []
    for g in range(PEER_ET // PEER_NKEYS):
        i1 = e * (PEER_ET // PEER_NKEYS) + g
        w = jnp.zeros((PEER_NKEYS, act.shape[1]), F32)
        for h in range(PEER_HEADS):
            s = a_ref[h, pl.ds(i1, 1), :] + b_ref[h]
            w = w + jnp.where(s >= tau_ref[h, 0:1, :], ea_ref[h, pl.ds(i1, 1), :] * eb_ref[h], 0.0)
        parts.append((w * act[g * PEER_NKEYS:(g + 1) * PEER_NKEYS]).astype(BF16))
    acc_scr[...] += _dot(vt_ref[...], jnp.concatenate(parts, axis=0))

    @pl.when(e == pl.num_programs(1) - 1)
    def _():
        o_ref[...] = acc_scr[...].T


def _peer_b(x_t, u, v_t, stage_a):
    d, t = x_t.shape
    n_exp = u.shape[0]
    tt = PEER_TT
    big = pl.BlockSpec((PEER_HEADS, PEER_NKEYS, tt), lambda i, e: (0, 0, i))
    return pl.pallas_call(
        _peer_b_kernel,
        grid=(t // tt, n_exp // PEER_ET),
        in_specs=[pl.BlockSpec((d, tt), lambda i, e: (0, i)),
                  pl.BlockSpec((PEER_ET, d), lambda i, e: (e, 0)),
                  pl.BlockSpec((d, PEER_ET), lambda i, e: (0, e)),
                  big, big, big, big,
                  pl.BlockSpec((PEER_HEADS, 8, tt), lambda i, e: (0, 0, i))],
        out_specs=pl.BlockSpec((tt, d), lambda i, e: (i, 0)),
        out_shape=jax.ShapeDtypeStruct((t, d), F32),
        scratch_shapes=[pltpu.VMEM((d, tt), F32)],
        compiler_params=_params(2),
        name="peer_experts",
    )(x_t, u, v_t, *stage_a)


def _resid_ln_kernel(x_ref, f_ref, g_ref, b_ref, y_ref, ybf_ref):
    y = _layer_norm(DEEPNORM_ALPHA * x_ref[...] + f_ref[...], g_ref[...], b_ref[...])
    y_ref[...] = y
    ybf_ref[...] = y.astype(BF16)


def _resid_ln(x, f, g, b):
    t, d = x.shape
    tm = 512
    row = pl.BlockSpec((tm, d), lambda i: (i, 0))
    vec = pl.BlockSpec((1, d), lambda i: (0, 0))
    return pl.pallas_call(
        _resid_ln_kernel,
        grid=(t // tm,),
        in_specs=[row, row, vec, vec],
        out_specs=[row, row],
        out_shape=[jax.ShapeDtypeStruct((t, d), F32), jax.ShapeDtypeStruct((t, d), BF16)],
        compiler_params=_params(1),
        name="resid_ln",
    )(x, f, g, b)


def _mixing_heads(p3, kvc, lam_params, diff_g, lam_init, consts):
    nsa_c, dil_c, sb_c = consts
    o_a = _nsa_attention(p3, kvc, nsa_c)
    o_b = _head_attention(_dil_kernel, "dilated_attention", p3, G_BQ, G_BK, G_BV, DIL_TQ, dil_c,
                          _softmax_scratch(DIL_TQ))
    diff_consts = (lam_params, diff_g.reshape(1, HEAD_DIM), jnp.full((1, 1), lam_init, F32))
    o_c = _head_attention(_diff_kernel, "diff_attention", p3, G_CQ, G_CK, G_CV, DIFF_TQ, diff_consts,
                          _softmax_scratch(2 * DIFF_TQ))
    o_d = _head_attention(_sb_kernel, "stick_breaking_attention", p3, G_DQ, G_DK, G_DV, SB_TQ, sb_c,
                          [pltpu.VMEM((SB_TQ, 1), F32), pltpu.VMEM((SB_TQ, LANES), F32)])
    return o_a, o_b, o_c, o_d


def kernel(x, w_in, w_out, nsa_cmp_pe, nsa_cmp_w1, nsa_cmp_w2, diff_lambda, diff_norm_g, ln1_g, ln1_b,
           peer_wq, peer_subkeys, peer_u, peer_v, ln2_g, ln2_b):
    b, seq, d = x.shape
    t = b * seq
    depth = w_in.shape[0]

    w_in_p = _prep_w_in(w_in)
    w_out_bf = w_out.astype(BF16)
    cmp_w1 = nsa_cmp_w1.astype(BF16)
    cmp_w2 = nsa_cmp_w2.astype(BF16)
    cmp_pe = jnp.broadcast_to(nsa_cmp_pe.reshape(depth, 2, 1, CMP_LEN * HEAD_DIM),
                              (depth, 2, 8, CMP_LEN * HEAD_DIM)).astype(BF16)
    wq_t = jnp.swapaxes(peer_wq, 1, 2).astype(BF16)
    sub_keys = peer_subkeys.reshape(depth, 2 * PEER_HEADS, PEER_NKEYS, -1).astype(BF16)
    u_bf = peer_u.astype(BF16)
    v_t = jnp.swapaxes(peer_v, 1, 2).astype(BF16)

    tables = _rope_tables(seq)
    consts = (_nsa_consts(seq), _dil_consts(), _sb_consts())

    xf = x.reshape(t, d)
    x_bf = xf.astype(BF16)
    for l in range(depth):
        lam_init = 0.8 - 0.6 * math.exp(-0.3 * l)
        proj = _project(x_bf, w_in_p[l], tables, seq)
        p3 = proj.reshape(b, seq, N_GROUPS * LANES)
        chunks = jnp.stack([p3[:, :, G_AKC * LANES:(G_AKC + 1) * LANES],
                            p3[:, :, G_AVC * LANES:(G_AVC + 1) * LANES]], axis=1)
        chunks = chunks.reshape(b, 2, seq // CMP_STRIDE, CMP_STRIDE * HEAD_DIM)
        kvc = _compress(chunks, cmp_w1[l], cmp_pe[l], cmp_w2[l])
        heads = _mixing_heads(p3, kvc, diff_lambda[l], diff_norm_g[l], lam_init, consts)
        heads = [o.reshape(t, -1) for o in heads]
        xf, x_bf = _outproj_ln(heads, w_out_bf[l], xf, ln1_g[l].reshape(1, d), ln1_b[l].reshape(1, d))
        x_t = x_bf.T
        stage_a = _peer_a(x_t, wq_t[l], sub_keys[l])
        ffn = _peer_b(x_t, u_bf[l], v_t[l], stage_a)
        xf, x_bf = _resid_ln(xf, ffn, ln2_g[l].reshape(1, d), ln2_b[l].reshape(1, d))
    return xf.reshape(b, seq, d)
```

```python
import functools
import math

import numpy as np
import jax
import jax.numpy as jnp
from jax import lax
from jax.experimental import pallas as pl
from jax.experimental.pallas import tpu as pltpu

F32 = jnp.float32
BF16 = jnp.bfloat16

D_MODEL = 2048
DEPTH = 4
HEAD_DIM = 128
ROPE_THETA = 500000.0
CMP_LEN = 32
CMP_STRIDE = 16
CMP_HIDDEN = 256
SLC_LEN = 64
SLC_TOPK = 16
WIN_LEN = 512
FORCE_BONUS = 1000.0
DILATED_PATTERNS = ((128, 1), (512, 4), (2048, 16))
DIFF_QK_DIM = 64
PEER_HEADS = 8
PEER_NKEYS = 128
PEER_TOPK = 16
LN_EPS = 1e-5
RMS_EPS = 1e-6
TINY = 1e-30
DEEPNORM_ALPHA = (2 * DEPTH) ** 0.25
NEG = -1e30

LANES = 128
VMEM_LIMIT_BYTES = 56 * 1024 * 1024

G_AQ, G_BQ, G_BK, G_AKC, G_AKS, G_AKW = 0, 4, 8, 12, 13, 14
G_CQ, G_CK = 16, 20
G_AVC, G_AVS, G_AVW, G_AG, G_BV, G_CV, G_DQ, G_DK, G_DV = 24, 25, 26, 27, 28, 32, 36, 40, 44
N_GROUPS = 48
PROJ_TN = 512
N_R128_TILES = 16 * LANES // PROJ_TN
N_R64_TILES = 8 * LANES // PROJ_TN
_IN_SEGMENTS = (
    (0, 512, G_AQ), (512, 128, G_AKC), (640, 128, G_AVC), (768, 128, G_AKS), (896, 128, G_AVS),
    (1024, 128, G_AKW), (1152, 128, G_AVW), (1280, 12, G_AG),
    (1292, 512, G_BQ), (1804, 512, G_BK), (2316, 512, G_BV),
    (2828, 512, G_CQ), (3340, 512, G_CK), (3852, 512, G_CV),
    (4364, 512, G_DQ), (4876, 512, G_DK), (5388, 512, G_DV),
)


def _params(n_axes):
    return pltpu.CompilerParams(dimension_semantics=("arbitrary",) * n_axes,
                                vmem_limit_bytes=VMEM_LIMIT_BYTES)


def _dot(a, b):
    return jnp.dot(a, b, preferred_element_type=F32)


def _dot_nt(a, b):
    return lax.dot_general(a, b, (((1,), (1,)), ((), ())), preferred_element_type=F32)


def _split_bf16(x):
    hi = x.astype(BF16)
    lo = (x - hi.astype(F32)).astype(BF16)
    return hi, lo


def _rope_groups(acc, c, s_lo, s_hi, half):
    outs = []
    for g in range(acc.shape[1] // LANES):
        xg = acc[:, g * LANES:(g + 1) * LANES]
        outs.append(xg * c + pltpu.roll(xg, LANES - half, 1) * s_lo + pltpu.roll(xg, half, 1) * s_hi)
    return jnp.concatenate(outs, axis=1)


def _proj_kernel(x_ref, w_ref, c128, lo128, hi128, c64, lo64, hi64, o_ref):
    j = pl.program_id(1)
    acc = _dot(x_ref[...], w_ref[...])

    @pl.when(j < N_R128_TILES)
    def _():
        o_ref[...] = _rope_groups(acc, c128[...], lo128[...], hi128[...], 16).astype(BF16)

    @pl.when((j >= N_R128_TILES) & (j < N_R128_TILES + N_R64_TILES))
    def _():
        o_ref[...] = _rope_groups(acc, c64[...], lo64[...], hi64[...], 8).astype(BF16)

    @pl.when(j >= N_R128_TILES + N_R64_TILES)
    def _():
        o_ref[...] = acc.astype(BF16)


def _project(x_bf, w_p, tables, seq):
    t, d = x_bf.shape
    n = w_p.shape[1]
    tm = min(1024, seq)
    n_pos = seq // tm
    tab_spec = pl.BlockSpec((tm, LANES), lambda i, j: (i % n_pos, 0))
    return pl.pallas_call(
        _proj_kernel,
        grid=(t // tm, n // PROJ_TN),
        in_specs=[pl.BlockSpec((tm, d), lambda i, j: (i, 0)),
                  pl.BlockSpec((d, PROJ_TN), lambda i, j: (0, j))] + [tab_spec] * 6,
        out_specs=pl.BlockSpec((tm, PROJ_TN), lambda i, j: (i, j)),
        out_shape=jax.ShapeDtypeStruct((t, n), BF16),
        compiler_params=_params(2),
        name="in_proj_rope",
    )(x_bf, w_p, *tables)


def _rope_tables(seq):
    tabs = []
    pos = jnp.arange(seq, dtype=F32)[:, None]
    for dh in (HEAD_DIM, DIFF_QK_DIM):
        rot = dh // 4
        half = rot // 2
        inv_freq = ROPE_THETA ** (-jnp.arange(half, dtype=F32) / half)
        ang = pos * inv_freq[None, :]
        cos, sin = jnp.cos(ang), jnp.sin(ang)
        ones = jnp.ones((seq, dh - rot), F32)
        zeros_h = jnp.zeros((seq, half), F32)
        zeros_r = jnp.zeros((seq, dh - rot), F32)
        c = jnp.concatenate([cos, cos, ones], axis=1)
        lo = jnp.concatenate([-sin, zeros_h, zeros_r], axis=1)
        hi = jnp.concatenate([zeros_h, sin, zeros_r], axis=1)
        reps = LANES // dh
        tabs += [jnp.tile(c, (1, reps)), jnp.tile(lo, (1, reps)), jnp.tile(hi, (1, reps))]
    return tabs


def _prep_w_in(w_in):
    nl, d, _ = w_in.shape
    cols = [None] * N_GROUPS
    for off, width, g in _IN_SEGMENTS:
        if width % LANES == 0:
            for k in range(width // LANES):
                cols[g + k] = w_in[:, :, off + k * LANES: off + (k + 1) * LANES]
        else:
            cols[g] = jnp.pad(w_in[:, :, off:off + width], ((0, 0), (0, 0), (0, LANES - width)))
    zero = jnp.zeros((nl, d, LANES), w_in.dtype)
    cols = [zero if c is None else c for c in cols]
    return jnp.concatenate(cols, axis=2).astype(BF16)


def _compress_kernel(ch_ref, w1_ref, pe_ref, w2_ref, o_ref):
    ch = ch_ref[0, 0]
    n_chunks, half_in = ch.shape
    first = _dot(ch, w1_ref[0, :half_in, :])
    second = _dot(ch, w1_ref[0, half_in:, :])
    pe_term = _dot(pe_ref[0], w1_ref[0])[0:1]
    hid = first + pltpu.roll(second, n_chunks - 1, 0) + pe_term
    hid = jax.nn.gelu(hid)
    o_ref[0, 0] = _dot(hid.astype(BF16), w2_ref[0]).astype(BF16)


def _compress(chunks, w1, pe, w2):
    b, _, n_chunks, width = chunks.shape
    return pl.pallas_call(
        _compress_kernel,
        grid=(b, 2),
        in_specs=[pl.BlockSpec((1, 1, n_chunks, width), lambda i, j: (i, j, 0, 0)),
                  pl.BlockSpec((1, 2 * width, CMP_HIDDEN), lambda i, j: (j, 0, 0)),
                  pl.BlockSpec((1, 8, 2 * width), lambda i, j: (j, 0, 0)),
                  pl.BlockSpec((1, CMP_HIDDEN, HEAD_DIM), lambda i, j: (j, 0, 0))],
        out_specs=pl.BlockSpec((1, 1, n_chunks, HEAD_DIM), lambda i, j: (i, j, 0, 0)),
        out_shape=jax.ShapeDtypeStruct((b, 2, n_chunks, HEAD_DIM), BF16),
        compiler_params=_params(2),
        name="nsa_compress",
    )(chunks, w1, pe, w2)


CHAIN = 128


def _softmax_step(s, valid, state, weight=None):
    m_scr, l_scr, _ = state
    if valid is not None:
        s = jnp.where(valid, s, NEG)
    m_old = m_scr[...]
    m_new = jnp.maximum(m_old, jnp.max(s, axis=1, keepdims=True))
    p = jnp.exp(s - m_new)
    if valid is not None:
        p = jnp.where(valid, p, 0.0)
    if weight is not None:
        p = p * weight
    alpha = jnp.exp(m_old - m_new)
    l_scr[...] = alpha * l_scr[...] + jnp.sum(p, axis=1, keepdims=True)
    m_scr[...] = m_new
    return p.astype(BF16), alpha


def _online_block(score_fns, valids, vblk, states, weights=None):
    scores = [fn() for fn in score_fns]
    steps = [_softmax_step(s, valids[c], states[c], None if weights is None else weights[c])
             for c, s in enumerate(scores)]
    for (p, alpha), (_, _, acc_scr) in zip(steps, states):
        acc_scr[...] = alpha * acc_scr[...] + _dot(p, vblk)


def _chain_states(scr):
    return [tuple(scr[3 * c:3 * c + 3]) for c in range(len(scr) // 3)]


def _reset(states):
    for m_scr, l_scr, acc_scr in states:
        m_scr[...] = jnp.full(m_scr.shape, NEG, F32)
        l_scr[...] = jnp.zeros(l_scr.shape, F32)
        acc_scr[...] = jnp.zeros(acc_scr.shape, F32)


def _normalized(states):
    return jnp.concatenate([acc[...] / l[...] for _, l, acc in states], axis=0)


NSA_TQ = 128
NSA_KB = 256
NSA_WB = 128
NSA_HEADS = 4


def _nsa_kernel(q_ref, kc_ref, vc_ref, ks_ref, vs_ref, kw_ref, vw_ref, g_ref, ovl_ref, exp_ref,
                o_ref, *scr):
    i = pl.program_id(1)
    states = _chain_states(scr)
    q0 = i * NSA_TQ
    scale = HEAD_DIM ** -0.5
    rows = NSA_HEADS * NSA_TQ
    q = q_ref[0]
    q4 = jnp.concatenate([q[:, h * LANES:(h + 1) * LANES] for h in range(NSA_HEADS)], axis=0)

    kc = kc_ref[0, 0]
    n_c = kc.shape[0]
    s = _dot_nt(q4, kc) * scale
    tpos4 = q0 + (lax.broadcasted_iota(jnp.int32, (rows, n_c), 0) & (NSA_TQ - 1))
    c_end = lax.broadcasted_iota(jnp.int32, (rows, n_c), 1) * CMP_STRIDE + (CMP_LEN - 1)
    vis = c_end <= tpos4
    s = jnp.where(vis, s, NEG)
    m = jnp.max(s, axis=1, keepdims=True)
    e = jnp.where(vis, jnp.exp(s - m), 0.0)
    p_cmp = e / jnp.maximum(jnp.sum(e, axis=1, keepdims=True), TINY)
    o_cmp = _dot(p_cmp.astype(BF16), vc_ref[0, 0])

    p_sum = p_cmp[0:NSA_TQ]
    for h in range(1, NSA_HEADS):
        p_sum = p_sum + p_cmp[h * NSA_TQ:(h + 1) * NSA_TQ]
    p_hi, p_lo = _split_bf16(p_sum)
    imp = _dot(p_hi, ovl_ref[...]) + _dot(p_lo, ovl_ref[...])
    j_idx = lax.broadcasted_iota(jnp.int32, (NSA_TQ, LANES), 1)
    t_blk = (q0 + lax.broadcasted_iota(jnp.int32, (NSA_TQ, LANES), 0)) // SLC_LEN
    forced = (j_idx == 0) | (j_idx == t_blk) | (j_idx == t_blk - 1)
    imp = jnp.where(forced, imp + FORCE_BONUS, imp)
    allowed = j_idx <= t_blk
    imp = jnp.where(allowed, imp, -jnp.inf)
    n_s = ks_ref.shape[1] // SLC_LEN
    rank = jnp.zeros((NSA_TQ, LANES), F32)
    for c in range(n_s):
        col = imp[:, c:c + 1]
        gt = jnp.where(col > imp, 1.0, 0.0)
        ge = jnp.where(col >= imp, 1.0, 0.0)
        rank = rank + jnp.where(j_idx > c, ge, gt)
    sel = jnp.where(allowed, jnp.where(rank < float(SLC_TOPK), 1.0, 0.0), 0.0).astype(BF16)

    def head_scores(h, kblk):
        return _dot_nt(q4[h * NSA_TQ:(h + 1) * NSA_TQ], kblk) * scale

    _reset(states)
    row_t = q0 + lax.broadcasted_iota(jnp.int32, (NSA_TQ, NSA_KB), 0)
    col_k = lax.broadcasted_iota(jnp.int32, (NSA_TQ, NSA_KB), 1)

    def slc_body(jb, carry):
        k0 = pl.multiple_of(jb * NSA_KB, NSA_KB)
        kblk = ks_ref[0, pl.ds(k0, NSA_KB), :]
        vblk = vs_ref[0, pl.ds(k0, NSA_KB), :]
        picked = _dot(sel, exp_ref[jb])
        valid = jnp.where(k0 + col_k <= row_t, picked, 0.0) > 0.5
        _online_block([functools.partial(head_scores, h, kblk) for h in range(NSA_HEADS)],
                      [valid] * NSA_HEADS, vblk, states)
        return carry

    lax.fori_loop(0, (q0 + NSA_TQ - 1) // NSA_KB + 1, slc_body, 0)
    o_slc = _normalized(states)

    _reset(states)
    row_w = q0 + lax.broadcasted_iota(jnp.int32, (NSA_TQ, NSA_WB), 0)
    col_w = lax.broadcasted_iota(jnp.int32, (NSA_TQ, NSA_WB), 1)

    def win_body(jb, carry):
        k0 = pl.multiple_of(jb * NSA_WB, NSA_WB)
        kblk = kw_ref[0, pl.ds(k0, NSA_WB), :]
        vblk = vw_ref[0, pl.ds(k0, NSA_WB), :]
        dist = row_w - (k0 + col_w)
        valid = (dist >= 0) & (dist <= WIN_LEN - 1)
        _online_block([functools.partial(head_scores, h, kblk) for h in range(NSA_HEADS)],
                      [valid] * NSA_HEADS, vblk, states)
        return carry

    first_w = jnp.maximum(q0 - WIN_LEN, 0) // NSA_WB
    lax.fori_loop(first_w, i * (NSA_TQ // NSA_WB) + NSA_TQ // NSA_WB, win_body, 0)
    o_win = _normalized(states)

    gate = jax.nn.sigmoid(g_ref[0].astype(F32))
    for h in range(NSA_HEADS):
        rs = slice(h * NSA_TQ, (h + 1) * NSA_TQ)
        o_h = (gate[:, h:h + 1] * o_cmp[rs] + gate[:, NSA_HEADS + h:NSA_HEADS + h + 1] * o_slc[rs]
               + gate[:, 2 * NSA_HEADS + h:2 * NSA_HEADS + h + 1] * o_win[rs])
        o_ref[0, :, h * LANES:(h + 1) * LANES] = o_h.astype(BF16)


def _nsa_consts(seq):
    n_chunks = seq // CMP_STRIDE
    n_s = seq // SLC_LEN
    c_start = np.arange(n_chunks)[:, None] * CMP_STRIDE
    j_start = np.arange(LANES)[None, :] * SLC_LEN
    real = (np.arange(n_chunks)[:, None] < n_chunks - 1) & (np.arange(LANES)[None, :] < n_s)
    ovl = ((c_start < j_start + SLC_LEN) & (c_start + CMP_LEN > j_start) & real).astype(np.float32)
    key_blk = (np.arange(seq) // SLC_LEN).reshape(seq // NSA_KB, 1, NSA_KB)
    expand = (key_blk == np.arange(LANES)[None, :, None]).astype(np.float32)
    return jnp.asarray(ovl, BF16), jnp.asarray(expand, BF16)


def _nsa_attention(p3, kvc, consts):
    b, seq, _ = p3.shape
    ovl, expand = consts
    n_chunks = kvc.shape[2]
    full = lambda g: pl.BlockSpec((1, seq, LANES), lambda bi, i, g=g: (bi, 0, g))
    return pl.pallas_call(
        _nsa_kernel,
        grid=(b, seq // NSA_TQ),
        in_specs=[pl.BlockSpec((1, NSA_TQ, NSA_HEADS * LANES), lambda bi, i: (bi, i, G_AQ // NSA_HEADS)),
                  pl.BlockSpec((1, 1, n_chunks, LANES), lambda bi, i: (bi, 0, 0, 0)),
                  pl.BlockSpec((1, 1, n_chunks, LANES), lambda bi, i: (bi, 1, 0, 0)),
                  full(G_AKS), full(G_AVS), full(G_AKW), full(G_AVW),
                  pl.BlockSpec((1, NSA_TQ, LANES), lambda bi, i: (bi, i, G_AG)),
                  pl.BlockSpec(ovl.shape, lambda bi, i: (0, 0)),
                  pl.BlockSpec(expand.shape, lambda bi, i: (0, 0, 0))],
        out_specs=pl.BlockSpec((1, NSA_TQ, NSA_HEADS * LANES), lambda bi, i: (bi, i, 0)),
        out_shape=jax.ShapeDtypeStruct((b, seq, NSA_HEADS * LANES), BF16),
        scratch_shapes=_softmax_scratch(NSA_HEADS * NSA_TQ),
        compiler_params=_params(2),
        name="nsa_attention",
    )(p3, kvc, kvc, p3, p3, p3, p3, p3, ovl, expand)


DIL_TQ = 256
DIL_KB = 256
DIL_SPAN = max(w for w, _ in DILATED_PATTERNS)


def _dil_kernel(q_ref, k_ref, v_ref, cnt_ref, add_ref, o_ref, *scr):
    i = pl.program_id(2)
    scale = HEAD_DIM ** -0.5
    q = q_ref[0]
    states = _chain_states(scr)
    _reset(states)

    def body(jb, carry):
        k0 = pl.multiple_of(jb * DIL_KB, DIL_KB)
        kblk = k_ref[0, pl.ds(k0, DIL_KB), :]
        vblk = v_ref[0, pl.ds(k0, DIL_KB), :]
        def scores(c):
            rs = slice(c * CHAIN, (c + 1) * CHAIN)
            return _dot_nt(q[rs], kblk) * scale + add_ref[i - jb, rs, :]

        n_chains = len(states)
        _online_block([functools.partial(scores, c) for c in range(n_chains)], [None] * n_chains, vblk, states,
                      [cnt_ref[i - jb, c * CHAIN:(c + 1) * CHAIN, :] for c in range(n_chains)])
        return carry

    lax.fori_loop(jnp.maximum(i - DIL_SPAN // DIL_KB, 0), i + 1, body, 0)
    o_ref[0] = _normalized(states).astype(BF16)


def _dil_consts():
    n_off = DIL_SPAN // DIL_KB + 1
    d = (np.arange(n_off)[:, None, None] * DIL_KB + np.arange(DIL_TQ)[None, :, None]
         - np.arange(DIL_KB)[None, None, :])
    cnt = np.zeros(d.shape, np.float32)
    for window, dil in DILATED_PATTERNS:
        cnt += ((d >= 0) & (d <= window) & (d % dil == 0)).astype(np.float32)
    add = np.where(cnt > 0, 0.0, NEG).astype(np.float32)
    return jnp.asarray(cnt), jnp.asarray(add)


def _head_attention(kernel, name, p3, gq, gk, gv, tq, consts, extra_scratch=()):
    b, seq, _ = p3.shape
    n_heads = 4
    const_specs = [pl.BlockSpec(c.shape, lambda bi, h, i, nd=c.ndim: (0,) * nd) for c in consts]
    return pl.pallas_call(
        kernel,
        grid=(b, n_heads, seq // tq),
        in_specs=[pl.BlockSpec((1, tq, LANES), lambda bi, h, i: (bi, i, gq + h)),
                  pl.BlockSpec((1, seq, LANES), lambda bi, h, i: (bi, 0, gk + h)),
                  pl.BlockSpec((1, seq, LANES), lambda bi, h, i: (bi, 0, gv + h))] + const_specs,
        out_specs=pl.BlockSpec((1, tq, LANES), lambda bi, h, i: (bi, i, h)),
        out_shape=jax.ShapeDtypeStruct((b, seq, n_heads * LANES), BF16),
        scratch_shapes=list(extra_scratch),
        compiler_params=_params(3),
        name=name,
    )(p3, p3, p3, *consts)


def _softmax_scratch(rows):
    one = [pltpu.VMEM((CHAIN, 1), F32), pltpu.VMEM((CHAIN, 1), F32), pltpu.VMEM((CHAIN, LANES), F32)]
    return one * (rows // CHAIN)


DIFF_TQ = 256
DIFF_KB = 256


def _diff_kernel(q_ref, k_ref, v_ref, lam_ref, g_ref, init_ref, o_ref, *scr):
    i = pl.program_id(2)
    q = q_ref[0] * (DIFF_QK_DIM ** -0.5)
    lane = lax.broadcasted_iota(jnp.int32, q.shape, 1)
    zero = jnp.zeros_like(q)
    q2 = jnp.concatenate([jnp.where(lane < DIFF_QK_DIM, q, zero), jnp.where(lane >= DIFF_QK_DIM, q, zero)],
                         axis=0)
    states = _chain_states(scr)
    _reset(states)

    def step(jb, masks):
        k0 = pl.multiple_of(jb * DIFF_KB, DIFF_KB)
        kblk = k_ref[0, pl.ds(k0, DIFF_KB), :]
        vblk = v_ref[0, pl.ds(k0, DIFF_KB), :]
        _online_block([functools.partial(_dot_nt, q2[c * CHAIN:(c + 1) * CHAIN], kblk)
                       for c in range(2 * DIFF_TQ // CHAIN)], masks, vblk, states)

    def body(jb, carry):
        step(jb, [None] * (2 * DIFF_TQ // CHAIN))
        return carry

    lax.fori_loop(0, i, body, 0)
    n_sub = DIFF_TQ // CHAIN
    diag = []
    for c in range(2 * n_sub):
        r_off = (c % n_sub) * CHAIN
        diag.append(lax.broadcasted_iota(jnp.int32, (CHAIN, DIFF_KB), 1)
                    <= lax.broadcasted_iota(jnp.int32, (CHAIN, DIFF_KB), 0) + r_off)
    step(i, diag)
    a = _normalized(states)
    lp = lam_ref[...]
    lam_init = init_ref[...]
    lam = (jnp.exp(jnp.sum(lp[0:1] * lp[1:2], axis=1, keepdims=True))
           - jnp.exp(jnp.sum(lp[2:3] * lp[3:4], axis=1, keepdims=True)) + lam_init)
    o = a[:DIFF_TQ] - lam * a[DIFF_TQ:]
    o = o * lax.rsqrt(jnp.mean(jnp.square(o), axis=1, keepdims=True) + RMS_EPS) * g_ref[...]
    o_ref[0] = (o * (1.0 - lam_init)).astype(BF16)


SB_TQ = 256
SB_KB = 256


def _sb_kernel(q_ref, k_ref, v_ref, tri_ref, o_ref, *scr):
    i = pl.program_id(2)
    scale = HEAD_DIM ** -0.5
    q = q_ref[0]
    states = [tuple(scr[2 * c:2 * c + 2]) for c in range(SB_TQ // CHAIN)]
    for run_scr, acc_scr in states:
        run_scr[...] = jnp.zeros(run_scr.shape, F32)
        acc_scr[...] = jnp.zeros(acc_scr.shape, F32)

    def step(jb, masks):
        k0 = pl.multiple_of(jb * SB_KB, SB_KB)
        kblk = k_ref[0, pl.ds(k0, SB_KB), :]
        vblk = v_ref[0, pl.ds(k0, SB_KB), :]
        zs = [_dot_nt(q[c * CHAIN:(c + 1) * CHAIN], kblk) * scale for c in range(len(states))]
        log_beta, log_1ms, splits = [], [], []
        for z, strict in zip(zs, masks):
            softplus = jnp.maximum(z, 0.0) + jnp.log(1.0 + jnp.exp(-jnp.abs(z)))
            log_1m = -softplus
            if strict is not None:
                log_1m = jnp.where(strict, log_1m, 0.0)
            log_beta.append(z - softplus)
            log_1ms.append(log_1m)
            splits.append(jnp.concatenate(_split_bf16(log_1m), axis=0))
        tails = [_dot(hl, tri_ref[...]) for hl in splits]
        weights = []
        for (run_scr, _), lb, log_1m, both, strict in zip(states, log_beta, log_1ms, tails, masks):
            tail = both[:CHAIN] + both[CHAIN:]
            run = run_scr[...]
            a = jnp.exp(lb + tail + run)
            if strict is not None:
                a = jnp.where(strict, a, 0.0)
            weights.append(a.astype(BF16))
            run_scr[...] = run + tail[:, 0:1] + log_1m[:, 0:1]
        for (_, acc_scr), a in zip(states, weights):
            acc_scr[...] += _dot(a, vblk)

    diag = [lax.broadcasted_iota(jnp.int32, (CHAIN, SB_KB), 1)
            < lax.broadcasted_iota(jnp.int32, (CHAIN, SB_KB), 0) + c * CHAIN for c in range(SB_TQ // CHAIN)]
    step(i, diag)

    def body(n, carry):
        step(i - 1 - n, [None] * (SB_TQ // CHAIN))
        return carry

    lax.fori_loop(0, i, body, 0)
    o_ref[0] = jnp.concatenate([acc[...] for _, acc in states], axis=0).astype(BF16)


def _sb_consts():
    j = np.arange(SB_KB)
    return (jnp.asarray((j[:, None] > j[None, :]).astype(np.float32), BF16),)


def _layer_norm(y, g, b):
    mu = jnp.mean(y, axis=1, keepdims=True)
    yc = y - mu
    var = jnp.mean(jnp.square(yc), axis=1, keepdims=True)
    return yc * lax.rsqrt(var + LN_EPS) * g + b


def _outproj_kernel(oa, ob, oc, od, w_ref, x_ref, g_ref, b_ref, y_ref, ybf_ref):
    width = oa.shape[1]
    mix = _dot(oa[...], w_ref[0:width, :])
    for k, o in enumerate((ob, oc, od), start=1):
        mix = mix + _dot(o[...], w_ref[k * width:(k + 1) * width, :])
    y = _layer_norm(DEEPNORM_ALPHA * x_ref[...] + mix, g_ref[...], b_ref[...])
    y_ref[...] = y
    ybf_ref[...] = y.astype(BF16)


def _outproj_ln(o_parts, w_out, x, g, b):
    t, d = x.shape
    tm = 512
    width = o_parts[0].shape[1]
    row = lambda w: pl.BlockSpec((tm, w), lambda i: (i, 0))
    vec = pl.BlockSpec((1, d), lambda i: (0, 0))
    return pl.pallas_call(
        _outproj_kernel,
        grid=(t // tm,),
        in_specs=[row(width)] * 4 + [pl.BlockSpec(w_out.shape, lambda i: (0, 0)), row(d), vec, vec],
        out_specs=[row(d), row(d)],
        out_shape=[jax.ShapeDtypeStruct((t, d), F32), jax.ShapeDtypeStruct((t, d), BF16)],
        compiler_params=_params(1),
        name="out_proj_ln",
    )(*o_parts, w_out, x, g, b)


PEER_TT = 512


def _top_values(x, n):
    out = []
    for r in range(n):
        mx = jnp.max(x, axis=0, keepdims=True)
        out.append(mx)
        if r + 1 < n:
            x = jnp.where(x == mx, -jnp.inf, x)
    return out


def _peer_a_kernel(xt_ref, wq_ref, sk_ref, thr_ref, ea_ref, b_ref, eb_ref, q_scr, cand_scr):
    q_scr[...] = _dot(wq_ref[...], xt_ref[...]).astype(BF16)
    n_pairs = cand_scr.shape[0]
    n_top = PEER_TOPK + 1

    def head(h, carry):
        r0 = pl.multiple_of(h * 2 * PEER_NKEYS, 2 * PEER_NKEYS)
        a = _dot(sk_ref[2 * h], q_scr[pl.ds(r0, PEER_NKEYS), :])
        b = _dot(sk_ref[2 * h + 1], q_scr[pl.ds(r0 + PEER_NKEYS, PEER_NKEYS), :])
        top_a = _top_values(a, n_top)
        top_b = _top_values(b, n_top)
        r = 0
        for ia in range(n_top):
            for ib in range(n_top // (ia + 1)):
                cand_scr[r:r + 1, :] = top_a[ia] + top_b[ib]
                r += 1
        cand_scr[r:n_pairs, :] = jnp.full((n_pairs - r, cand_scr.shape[1]), -jnp.inf, F32)
        top_s = _top_values(cand_scr[...], n_top)
        z = jnp.ones_like(top_s[0])
        for t in top_s[1:PEER_TOPK]:
            z = z + jnp.exp(t - top_s[0])
        tau = 0.5 * (top_s[PEER_TOPK - 1] + top_s[PEER_TOPK])
        thr_ref[h] = tau - a
        ea_ref[h] = jnp.exp(a - top_a[0]) / z
        b_ref[h] = b
        eb_ref[h] = jnp.exp(b - top_b[0])
        return carry

    lax.fori_loop(0, PEER_HEADS, head, 0)


def _peer_a(x_t, wq_t, sub_keys):
    d, t = x_t.shape
    tt = PEER_TT
    big = pl.BlockSpec((PEER_HEADS, PEER_NKEYS, tt), lambda i: (0, 0, i))
    big_shape = jax.ShapeDtypeStruct((PEER_HEADS, PEER_NKEYS, t), F32)
    n_pairs = 56
    return pl.pallas_call(
        _peer_a_kernel,
        grid=(t // tt,),
        in_specs=[pl.BlockSpec((d, tt), lambda i: (0, i)),
                  pl.BlockSpec(wq_t.shape, lambda i: (0, 0)),
                  pl.BlockSpec(sub_keys.shape, lambda i: (0, 0, 0))],
        out_specs=[big, big, big, big],
        out_shape=[big_shape] * 4,
        scratch_shapes=[pltpu.VMEM((wq_t.shape[0], tt), BF16), pltpu.VMEM((n_pairs, tt), F32)],
        compiler_params=_params(1),
        name="peer_scores_topk",
    )(x_t, wq_t, sub_keys)


PEER_ET = 1024
PEER_TC = 256


def _peer_b_kernel(xt_ref, u_ref, vt_ref, thr_ref, ea_ref, b_ref, eb_ref, o_ref, acc_scr, w_scr):
    e = pl.program_id(1)
    n_g = PEER_ET // PEER_NKEYS
    chunks = [slice(c * PEER_TC, (c + 1) * PEER_TC) for c in range(xt_ref.shape[1] // PEER_TC)]

    @pl.when(e == 0)
    def _():
        acc_scr[...] = jnp.zeros(acc_scr.shape, F32)

    hidden = [_dot(u_ref[...], xt_ref[:, cs]) for cs in chunks]
    for g in range(n_g):
        i1 = e * n_g + g
        thr_rows = [thr_ref[h, pl.ds(i1, 1), :] for h in range(PEER_HEADS)]
        ea_rows = [ea_ref[h, pl.ds(i1, 1), :] for h in range(PEER_HEADS)]
        for c in range(xt_ref.shape[1] // LANES):
            ls = slice(c * LANES, (c + 1) * LANES)
            w = jnp.zeros((PEER_NKEYS, LANES), F32)
            for h in range(PEER_HEADS):
                w = w + jnp.where(b_ref[h, :, ls] >= thr_rows[h][:, ls], ea_rows[h][:, ls] * eb_ref[h, :, ls], 0.0)
            w_scr[g * PEER_NKEYS:(g + 1) * PEER_NKEYS, ls] = w
    for cs, hid in zip(chunks, hidden):
        p = (w_scr[:, cs] * jax.nn.gelu(hid)).astype(BF16)
        acc_scr[:, cs] += _dot(vt_ref[...], p)

    @pl.when(e == pl.num_programs(1) - 1)
    def _():
        o_ref[...] = acc_scr[...].T


def _peer_b(x_t, u, v_t, stage_a):
    d, t = x_t.shape
    n_exp = u.shape[0]
    tt = PEER_TT
    big = pl.BlockSpec((PEER_HEADS, PEER_NKEYS, tt), lambda i, e: (0, 0, i))
    return pl.pallas_call(
        _peer_b_kernel,
        grid=(t // tt, n_exp // PEER_ET),
        in_specs=[pl.BlockSpec((d, tt), lambda i, e: (0, i)),
                  pl.BlockSpec((PEER_ET, d), lambda i, e: (e, 0)),
                  pl.BlockSpec((d, PEER_ET), lambda i, e: (0, e)),
                  big, big, big, big],
        out_specs=pl.BlockSpec((tt, d), lambda i, e: (i, 0)),
        out_shape=jax.ShapeDtypeStruct((t, d), F32),
        scratch_shapes=[pltpu.VMEM((d, tt), F32), pltpu.VMEM((PEER_ET, tt), F32)],
        compiler_params=_params(2),
        name="peer_experts",
    )(x_t, u, v_t, *stage_a)


def _resid_ln_kernel(x_ref, f_ref, g_ref, b_ref, y_ref, ybf_ref):
    y = _layer_norm(DEEPNORM_ALPHA * x_ref[...] + f_ref[...], g_ref[...], b_ref[...])
    y_ref[...] = y
    ybf_ref[...] = y.astype(BF16)


def _resid_ln(x, f, g, b):
    t, d = x.shape
    tm = 512
    row = pl.BlockSpec((tm, d), lambda i: (i, 0))
    vec = pl.BlockSpec((1, d), lambda i: (0, 0))
    return pl.pallas_call(
        _resid_ln_kernel,
        grid=(t // tm,),
        in_specs=[row, row, vec, vec],
        out_specs=[row, row],
        out_shape=[jax.ShapeDtypeStruct((t, d), F32), jax.ShapeDtypeStruct((t, d), BF16)],
        compiler_params=_params(1),
        name="resid_ln",
    )(x, f, g, b)


def _mixing_heads(p3, kvc, lam_params, diff_g, lam_init, consts):
    nsa_c, dil_c, sb_c = consts
    o_a = _nsa_attention(p3, kvc, nsa_c)
    o_b = _head_attention(_dil_kernel, "dilated_attention", p3, G_BQ, G_BK, G_BV, DIL_TQ, dil_c,
                          _softmax_scratch(DIL_TQ))
    diff_consts = (lam_params, diff_g.reshape(1, HEAD_DIM), jnp.full((1, 1), lam_init, F32))
    o_c = _head_attention(_diff_kernel, "diff_attention", p3, G_CQ, G_CK, G_CV, DIFF_TQ, diff_consts,
                          _softmax_scratch(2 * DIFF_TQ))
    o_d = _head_attention(_sb_kernel, "stick_breaking_attention", p3, G_DQ, G_DK, G_DV, SB_TQ, sb_c,
                          [pltpu.VMEM((CHAIN, 1), F32), pltpu.VMEM((CHAIN, LANES), F32)] * (SB_TQ // CHAIN))
    return o_a, o_b, o_c, o_d


def kernel(x, w_in, w_out, nsa_cmp_pe, nsa_cmp_w1, nsa_cmp_w2, diff_lambda, diff_norm_g, ln1_g, ln1_b,
           peer_wq, peer_subkeys, peer_u, peer_v, ln2_g, ln2_b):
    b, seq, d = x.shape
    t = b * seq
    depth = w_in.shape[0]

    w_in_p = _prep_w_in(w_in)
    w_out_bf = w_out.astype(BF16)
    cmp_w1 = nsa_cmp_w1.astype(BF16)
    cmp_w2 = nsa_cmp_w2.astype(BF16)
    cmp_pe = jnp.broadcast_to(nsa_cmp_pe.reshape(depth, 2, 1, CMP_LEN * HEAD_DIM),
                              (depth, 2, 8, CMP_LEN * HEAD_DIM)).astype(BF16)
    wq_t = jnp.swapaxes(peer_wq, 1, 2).astype(BF16)
    sub_keys = peer_subkeys.reshape(depth, 2 * PEER_HEADS, PEER_NKEYS, -1).astype(BF16)
    u_bf = peer_u.astype(BF16)
    v_t = jnp.swapaxes(peer_v, 1, 2).astype(BF16)

    tables = _rope_tables(seq)
    consts = (_nsa_consts(seq), _dil_consts(), _sb_consts())

    xf = x.reshape(t, d)
    x_bf = xf.astype(BF16)
    for l in range(depth):
        lam_init = 0.8 - 0.6 * math.exp(-0.3 * l)
        proj = _project(x_bf, w_in_p[l], tables, seq)
        p3 = proj.reshape(b, seq, N_GROUPS * LANES)
        chunks = jnp.stack([p3[:, :, G_AKC * LANES:(G_AKC + 1) * LANES],
                            p3[:, :, G_AVC * LANES:(G_AVC + 1) * LANES]], axis=1)
        chunks = chunks.reshape(b, 2, seq // CMP_STRIDE, CMP_STRIDE * HEAD_DIM)
        kvc = _compress(chunks, cmp_w1[l], cmp_pe[l], cmp_w2[l])
        heads = _mixing_heads(p3, kvc, diff_lambda[l], diff_norm_g[l], lam_init, consts)
        heads = [o.reshape(t, -1) for o in heads]
        xf, x_bf = _outproj_ln(heads, w_out_bf[l], xf, ln1_g[l].reshape(1, d), ln1_b[l].reshape(1, d))
        x_t = x_bf.T
        stage_a = _peer_a(x_t, wq_t[l], sub_keys[l])
        ffn = _peer_b(x_t, u_bf[l], v_t[l], stage_a)
        xf, x_bf = _resid_ln(xf, ffn, ln2_g[l].reshape(1, d), ln2_b[l].reshape(1, d))
    return xf.reshape(b, seq, d)
```

```python
import functools
import math

import numpy as np
import jax
import jax.numpy as jnp
from jax import lax
from jax.experimental import pallas as pl
from jax.experimental.pallas import tpu as pltpu

F32 = jnp.float32
BF16 = jnp.bfloat16

D_MODEL = 2048
DEPTH = 4
HEAD_DIM = 128
ROPE_THETA = 500000.0
CMP_LEN = 32
CMP_STRIDE = 16
CMP_HIDDEN = 256
SLC_LEN = 64
SLC_TOPK = 16
WIN_LEN = 512
FORCE_BONUS = 1000.0
DILATED_PATTERNS = ((128, 1), (512, 4), (2048, 16))
DIFF_QK_DIM = 64
PEER_HEADS = 8
PEER_NKEYS = 128
PEER_TOPK = 16
LN_EPS = 1e-5
RMS_EPS = 1e-6
TINY = 1e-30
DEEPNORM_ALPHA = (2 * DEPTH) ** 0.25
NEG = -1e30

LANES = 128
VMEM_LIMIT_BYTES = 56 * 1024 * 1024

G_AQ, G_BQ, G_BK, G_AKC, G_AKS, G_AKW = 0, 4, 8, 12, 13, 14
G_CQ, G_CK = 16, 20
G_AVC, G_AVS, G_AVW, G_AG, G_BV, G_CV, G_DQ, G_DK, G_DV = 24, 25, 26, 27, 28, 32, 36, 40, 44
N_GROUPS = 48
PROJ_TN = 512
N_R128_TILES = 16 * LANES // PROJ_TN
N_R64_TILES = 8 * LANES // PROJ_TN
_IN_SEGMENTS = (
    (0, 512, G_AQ), (512, 128, G_AKC), (640, 128, G_AVC), (768, 128, G_AKS), (896, 128, G_AVS),
    (1024, 128, G_AKW), (1152, 128, G_AVW), (1280, 12, G_AG),
    (1292, 512, G_BQ), (1804, 512, G_BK), (2316, 512, G_BV),
    (2828, 512, G_CQ), (3340, 512, G_CK), (3852, 512, G_CV),
    (4364, 512, G_DQ), (4876, 512, G_DK), (5388, 512, G_DV),
)


def _params(n_axes):
    return pltpu.CompilerParams(dimension_semantics=("arbitrary",) * n_axes,
                                vmem_limit_bytes=VMEM_LIMIT_BYTES)


def _dot(a, b):
    return jnp.dot(a, b, preferred_element_type=F32)


def _dot_nt(a, b):
    return lax.dot_general(a, b, (((1,), (1,)), ((), ())), preferred_element_type=F32)


def _split_bf16(x):
    hi = x.astype(BF16)
    lo = (x - hi.astype(F32)).astype(BF16)
    return hi, lo


def _rope_groups(acc, c, s_lo, s_hi, half):
    outs = []
    for g in range(acc.shape[1] // LANES):
        xg = acc[:, g * LANES:(g + 1) * LANES]
        outs.append(xg * c + pltpu.roll(xg, LANES - half, 1) * s_lo + pltpu.roll(xg, half, 1) * s_hi)
    return jnp.concatenate(outs, axis=1)


def _proj_kernel(x_ref, w_ref, c128, lo128, hi128, c64, lo64, hi64, o_ref):
    j = pl.program_id(1)

    @pl.when(j < N_R128_TILES)
    def _():
        acc = _dot(x_ref[...], w_ref[...])
        o_ref[...] = _rope_groups(acc, c128[...], lo128[...], hi128[...], 16).astype(BF16)

    @pl.when((j >= N_R128_TILES) & (j < N_R128_TILES + N_R64_TILES))
    def _():
        acc = _dot(x_ref[...], w_ref[...])
        o_ref[...] = _rope_groups(acc, c64[...], lo64[...], hi64[...], 8).astype(BF16)

    @pl.when(j >= N_R128_TILES + N_R64_TILES)
    def _():
        o_ref[...] = _dot(x_ref[...], w_ref[...]).astype(BF16)


def _project(x_bf, w_p, tables, seq):
    t, d = x_bf.shape
    n = w_p.shape[1]
    tm = min(1024, seq)
    n_pos = seq // tm
    tab_spec = pl.BlockSpec((tm, LANES), lambda i, j: (i % n_pos, 0))
    return pl.pallas_call(
        _proj_kernel,
        grid=(t // tm, n // PROJ_TN),
        in_specs=[pl.BlockSpec((tm, d), lambda i, j: (i, 0)),
                  pl.BlockSpec((d, PROJ_TN), lambda i, j: (0, j))] + [tab_spec] * 6,
        out_specs=pl.BlockSpec((tm, PROJ_TN), lambda i, j: (i, j)),
        out_shape=jax.ShapeDtypeStruct((t, n), BF16),
        compiler_params=_params(2),
        name="in_proj_rope",
    )(x_bf, w_p, *tables)


def _rope_tables(seq):
    tabs = []
    pos = jnp.arange(seq, dtype=F32)[:, None]
    for dh in (HEAD_DIM, DIFF_QK_DIM):
        rot = dh // 4
        half = rot // 2
        inv_freq = ROPE_THETA ** (-jnp.arange(half, dtype=F32) / half)
        ang = pos * inv_freq[None, :]
        cos, sin = jnp.cos(ang), jnp.sin(ang)
        ones = jnp.ones((seq, dh - rot), F32)
        zeros_h = jnp.zeros((seq, half), F32)
        zeros_r = jnp.zeros((seq, dh - rot), F32)
        c = jnp.concatenate([cos, cos, ones], axis=1)
        lo = jnp.concatenate([-sin, zeros_h, zeros_r], axis=1)
        hi = jnp.concatenate([zeros_h, sin, zeros_r], axis=1)
        reps = LANES // dh
        tabs += [jnp.tile(c, (1, reps)), jnp.tile(lo, (1, reps)), jnp.tile(hi, (1, reps))]
    return tabs


def _prep_w_in(w_in):
    nl, d, _ = w_in.shape
    cols = [None] * N_GROUPS
    for off, width, g in _IN_SEGMENTS:
        if width % LANES == 0:
            for k in range(width // LANES):
                cols[g + k] = w_in[:, :, off + k * LANES: off + (k + 1) * LANES]
        else:
            cols[g] = jnp.pad(w_in[:, :, off:off + width], ((0, 0), (0, 0), (0, LANES - width)))
    zero = jnp.zeros((nl, d, LANES), w_in.dtype)
    cols = [zero if c is None else c for c in cols]
    return jnp.concatenate(cols, axis=2).astype(BF16)


def _compress_kernel(ch_ref, w1_ref, pe_ref, w2_ref, o_ref):
    ch = ch_ref[0, 0]
    n_chunks, half_in = ch.shape
    first = _dot(ch, w1_ref[0, :half_in, :])
    second = _dot(ch, w1_ref[0, half_in:, :])
    pe_term = _dot(pe_ref[0], w1_ref[0])[0:1]
    hid = first + pltpu.roll(second, n_chunks - 1, 0) + pe_term
    hid = jax.nn.gelu(hid)
    o_ref[0, 0] = _dot(hid.astype(BF16), w2_ref[0]).astype(BF16)


def _compress(chunks, w1, pe, w2):
    b, _, n_chunks, width = chunks.shape
    return pl.pallas_call(
        _compress_kernel,
        grid=(b, 2),
        in_specs=[pl.BlockSpec((1, 1, n_chunks, width), lambda i, j: (i, j, 0, 0)),
                  pl.BlockSpec((1, 2 * width, CMP_HIDDEN), lambda i, j: (j, 0, 0)),
                  pl.BlockSpec((1, 8, 2 * width), lambda i, j: (j, 0, 0)),
                  pl.BlockSpec((1, CMP_HIDDEN, HEAD_DIM), lambda i, j: (j, 0, 0))],
        out_specs=pl.BlockSpec((1, 1, n_chunks, HEAD_DIM), lambda i, j: (i, j, 0, 0)),
        out_shape=jax.ShapeDtypeStruct((b, 2, n_chunks, HEAD_DIM), BF16),
        compiler_params=_params(2),
        name="nsa_compress",
    )(chunks, w1, pe, w2)


CHAIN = 128


def _softmax_step(s, valid, state, weight=None):
    m_scr, l_scr, _ = state
    if valid is not None:
        s = jnp.where(valid, s, NEG)
    m_old = m_scr[...]
    m_new = jnp.maximum(m_old, jnp.max(s, axis=1, keepdims=True))
    p = jnp.exp(s - m_new)
    if valid is not None:
        p = jnp.where(valid, p, 0.0)
    if weight is not None:
        p = p * weight
    alpha = jnp.exp(m_old - m_new)
    l_scr[...] = alpha * l_scr[...] + jnp.sum(p, axis=1, keepdims=True)
    m_scr[...] = m_new
    return p.astype(BF16), alpha


def _online_block(score_fns, valids, vblk, states, weights=None):
    scores = [fn() for fn in score_fns]
    steps = [_softmax_step(s, valids[c], states[c], None if weights is None else weights[c])
             for c, s in enumerate(scores)]
    for (p, alpha), (_, _, acc_scr) in zip(steps, states):
        acc_scr[...] = alpha * acc_scr[...] + _dot(p, vblk)


def _chain_states(scr):
    return [tuple(scr[3 * c:3 * c + 3]) for c in range(len(scr) // 3)]


def _reset(states):
    for m_scr, l_scr, acc_scr in states:
        m_scr[...] = jnp.full(m_scr.shape, NEG, F32)
        l_scr[...] = jnp.zeros(l_scr.shape, F32)
        acc_scr[...] = jnp.zeros(acc_scr.shape, F32)


def _normalized(states):
    return jnp.concatenate([acc[...] / l[...] for _, l, acc in states], axis=0)


NSA_TQ = 128
NSA_KB = 256
NSA_WB = 128
NSA_HEADS = 4


def _nsa_kernel(q_ref, kc_ref, vc_ref, ks_ref, vs_ref, kw_ref, vw_ref, g_ref, ovl_ref, exp_ref,
                o_ref, *scr):
    i = pl.program_id(1)
    states = _chain_states(scr)
    q0 = i * NSA_TQ
    scale = HEAD_DIM ** -0.5
    rows = NSA_HEADS * NSA_TQ
    q = q_ref[0]
    q4 = jnp.concatenate([q[:, h * LANES:(h + 1) * LANES] for h in range(NSA_HEADS)], axis=0)

    kc = kc_ref[0, 0]
    n_c = kc.shape[0]
    s = _dot_nt(q4, kc) * scale
    tpos4 = q0 + (lax.broadcasted_iota(jnp.int32, (rows, n_c), 0) & (NSA_TQ - 1))
    c_end = lax.broadcasted_iota(jnp.int32, (rows, n_c), 1) * CMP_STRIDE + (CMP_LEN - 1)
    vis = c_end <= tpos4
    s = jnp.where(vis, s, NEG)
    m = jnp.max(s, axis=1, keepdims=True)
    e = jnp.where(vis, jnp.exp(s - m), 0.0)
    p_cmp = e / jnp.maximum(jnp.sum(e, axis=1, keepdims=True), TINY)
    o_cmp = _dot(p_cmp.astype(BF16), vc_ref[0, 0])

    p_sum = p_cmp[0:NSA_TQ]
    for h in range(1, NSA_HEADS):
        p_sum = p_sum + p_cmp[h * NSA_TQ:(h + 1) * NSA_TQ]
    p_hi, p_lo = _split_bf16(p_sum)
    imp = _dot(p_hi, ovl_ref[...]) + _dot(p_lo, ovl_ref[...])
    j_idx = lax.broadcasted_iota(jnp.int32, (NSA_TQ, LANES), 1)
    t_blk = (q0 + lax.broadcasted_iota(jnp.int32, (NSA_TQ, LANES), 0)) // SLC_LEN
    forced = (j_idx == 0) | (j_idx == t_blk) | (j_idx == t_blk - 1)
    imp = jnp.where(forced, imp + FORCE_BONUS, imp)
    allowed = j_idx <= t_blk
    imp = jnp.where(allowed, imp, -jnp.inf)
    n_s = ks_ref.shape[1] // SLC_LEN
    rank = jnp.zeros((NSA_TQ, LANES), F32)
    for c in range(n_s):
        col = imp[:, c:c + 1]
        gt = jnp.where(col > imp, 1.0, 0.0)
        ge = jnp.where(col >= imp, 1.0, 0.0)
        rank = rank + jnp.where(j_idx > c, ge, gt)
    sel = jnp.where(allowed, jnp.where(rank < float(SLC_TOPK), 1.0, 0.0), 0.0).astype(BF16)

    def head_scores(h, kblk):
        return _dot_nt(q4[h * NSA_TQ:(h + 1) * NSA_TQ], kblk) * scale

    _reset(states)
    row_t = q0 + lax.broadcasted_iota(jnp.int32, (NSA_TQ, NSA_KB), 0)
    col_k = lax.broadcasted_iota(jnp.int32, (NSA_TQ, NSA_KB), 1)

    def slc_body(jb, carry):
        k0 = pl.multiple_of(jb * NSA_KB, NSA_KB)
        kblk = ks_ref[0, pl.ds(k0, NSA_KB), :]
        vblk = vs_ref[0, pl.ds(k0, NSA_KB), :]
        picked = _dot(sel, exp_ref[jb])
        valid = jnp.where(k0 + col_k <= row_t, picked, 0.0) > 0.5
        _online_block([functools.partial(head_scores, h, kblk) for h in range(NSA_HEADS)],
                      [valid] * NSA_HEADS, vblk, states)
        return carry

    lax.fori_loop(0, (q0 + NSA_TQ - 1) // NSA_KB + 1, slc_body, 0)
    o_slc = _normalized(states)

    _reset(states)
    row_w = q0 + lax.broadcasted_iota(jnp.int32, (NSA_TQ, NSA_WB), 0)
    col_w = lax.broadcasted_iota(jnp.int32, (NSA_TQ, NSA_WB), 1)

    def win_body(jb, carry):
        k0 = pl.multiple_of(jb * NSA_WB, NSA_WB)
        kblk = kw_ref[0, pl.ds(k0, NSA_WB), :]
        vblk = vw_ref[0, pl.ds(k0, NSA_WB), :]
        dist = row_w - (k0 + col_w)
        valid = (dist >= 0) & (dist <= WIN_LEN - 1)
        _online_block([functools.partial(head_scores, h, kblk) for h in range(NSA_HEADS)],
                      [valid] * NSA_HEADS, vblk, states)
        return carry

    first_w = jnp.maximum(q0 - WIN_LEN, 0) // NSA_WB
    lax.fori_loop(first_w, i * (NSA_TQ // NSA_WB) + NSA_TQ // NSA_WB, win_body, 0)
    o_win = _normalized(states)

    gate = jax.nn.sigmoid(g_ref[0].astype(F32))
    for h in range(NSA_HEADS):
        rs = slice(h * NSA_TQ, (h + 1) * NSA_TQ)
        o_h = (gate[:, h:h + 1] * o_cmp[rs] + gate[:, NSA_HEADS + h:NSA_HEADS + h + 1] * o_slc[rs]
               + gate[:, 2 * NSA_HEADS + h:2 * NSA_HEADS + h + 1] * o_win[rs])
        o_ref[0, :, h * LANES:(h + 1) * LANES] = o_h.astype(BF16)


def _nsa_consts(seq):
    n_chunks = seq // CMP_STRIDE
    n_s = seq // SLC_LEN
    c_start = np.arange(n_chunks)[:, None] * CMP_STRIDE
    j_start = np.arange(LANES)[None, :] * SLC_LEN
    real = (np.arange(n_chunks)[:, None] < n_chunks - 1) & (np.arange(LANES)[None, :] < n_s)
    ovl = ((c_start < j_start + SLC_LEN) & (c_start + CMP_LEN > j_start) & real).astype(np.float32)
    key_blk = (np.arange(seq) // SLC_LEN).reshape(seq // NSA_KB, 1, NSA_KB)
    expand = (key_blk == np.arange(LANES)[None, :, None]).astype(np.float32)
    return jnp.asarray(ovl, BF16), jnp.asarray(expand, BF16)


def _nsa_attention(p3, kvc, consts):
    b, seq, _ = p3.shape
    ovl, expand = consts
    n_chunks = kvc.shape[2]
    full = lambda g: pl.BlockSpec((1, seq, LANES), lambda bi, i, g=g: (bi, 0, g))
    return pl.pallas_call(
        _nsa_kernel,
        grid=(b, seq // NSA_TQ),
        in_specs=[pl.BlockSpec((1, NSA_TQ, NSA_HEADS * LANES), lambda bi, i: (bi, i, G_AQ // NSA_HEADS)),
                  pl.BlockSpec((1, 1, n_chunks, LANES), lambda bi, i: (bi, 0, 0, 0)),
                  pl.BlockSpec((1, 1, n_chunks, LANES), lambda bi, i: (bi, 1, 0, 0)),
                  full(G_AKS), full(G_AVS), full(G_AKW), full(G_AVW),
                  pl.BlockSpec((1, NSA_TQ, LANES), lambda bi, i: (bi, i, G_AG)),
                  pl.BlockSpec(ovl.shape, lambda bi, i: (0, 0)),
                  pl.BlockSpec(expand.shape, lambda bi, i: (0, 0, 0))],
        out_specs=pl.BlockSpec((1, NSA_TQ, NSA_HEADS * LANES), lambda bi, i: (bi, i, 0)),
        out_shape=jax.ShapeDtypeStruct((b, seq, NSA_HEADS * LANES), BF16),
        scratch_shapes=_softmax_scratch(NSA_HEADS * NSA_TQ),
        compiler_params=_params(2),
        name="nsa_attention",
    )(p3, kvc, kvc, p3, p3, p3, p3, p3, ovl, expand)


DIL_TQ = 512
DIL_KB = 256
DIL_SPAN = max(w for w, _ in DILATED_PATTERNS)


def _dil_kernel(q_ref, k_ref, v_ref, cnt_ref, add_ref, o_ref, *scr):
    i = pl.program_id(2)
    scale = HEAD_DIM ** -0.5
    q = q_ref[0]
    states = _chain_states(scr)
    _reset(states)
    ratio = DIL_TQ // DIL_KB

    def body(jb, carry):
        k0 = pl.multiple_of(jb * DIL_KB, DIL_KB)
        kblk = k_ref[0, pl.ds(k0, DIL_KB), :]
        vblk = v_ref[0, pl.ds(k0, DIL_KB), :]
        off = i * ratio - jb + (ratio - 1)

        def scores(c):
            rs = slice(c * CHAIN, (c + 1) * CHAIN)
            return _dot_nt(q[rs], kblk) * scale + add_ref[off, rs, :]

        n_chains = len(states)
        _online_block([functools.partial(scores, c) for c in range(n_chains)], [None] * n_chains, vblk, states,
                      [cnt_ref[off, c * CHAIN:(c + 1) * CHAIN, :] for c in range(n_chains)])
        return carry

    lax.fori_loop(jnp.maximum(i * ratio - DIL_SPAN // DIL_KB, 0), (i + 1) * ratio, body, 0)
    o_ref[0] = _normalized(states).astype(BF16)


def _dil_consts():
    ratio = DIL_TQ // DIL_KB
    n_off = DIL_SPAN // DIL_KB + ratio
    d = ((np.arange(n_off)[:, None, None] - (ratio - 1)) * DIL_KB + np.arange(DIL_TQ)[None, :, None]
         - np.arange(DIL_KB)[None, None, :])
    cnt = np.zeros(d.shape, np.float32)
    for window, dil in DILATED_PATTERNS:
        cnt += ((d >= 0) & (d <= window) & (d % dil == 0)).astype(np.float32)
    add = np.where(cnt > 0, 0.0, NEG).astype(np.float32)
    return jnp.asarray(cnt), jnp.asarray(add)


def _head_attention(kernel, name, p3, gq, gk, gv, tq, consts, extra_scratch=()):
    b, seq, _ = p3.shape
    n_heads = 4
    const_specs = [pl.BlockSpec(c.shape, lambda bi, h, i, nd=c.ndim: (0,) * nd) for c in consts]
    return pl.pallas_call(
        kernel,
        grid=(b, n_heads, seq // tq),
        in_specs=[pl.BlockSpec((1, tq, LANES), lambda bi, h, i: (bi, i, gq + h)),
                  pl.BlockSpec((1, seq, LANES), lambda bi, h, i: (bi, 0, gk + h)),
                  pl.BlockSpec((1, seq, LANES), lambda bi, h, i: (bi, 0, gv + h))] + const_specs,
        out_specs=pl.BlockSpec((1, tq, LANES), lambda bi, h, i: (bi, i, h)),
        out_shape=jax.ShapeDtypeStruct((b, seq, n_heads * LANES), BF16),
        scratch_shapes=list(extra_scratch),
        compiler_params=_params(3),
        name=name,
    )(p3, p3, p3, *consts)


def _softmax_scratch(rows):
    one = [pltpu.VMEM((CHAIN, 1), F32), pltpu.VMEM((CHAIN, 1), F32), pltpu.VMEM((CHAIN, LANES), F32)]
    return one * (rows // CHAIN)


DIFF_TQ = 256
DIFF_KB = 256


def _diff_kernel(q_ref, k_ref, v_ref, lam_ref, g_ref, init_ref, o_ref, *scr):
    i = pl.program_id(2)
    q = q_ref[0] * (DIFF_QK_DIM ** -0.5)
    lane = lax.broadcasted_iota(jnp.int32, q.shape, 1)
    zero = jnp.zeros_like(q)
    q2 = jnp.concatenate([jnp.where(lane < DIFF_QK_DIM, q, zero), jnp.where(lane >= DIFF_QK_DIM, q, zero)],
                         axis=0)
    states = _chain_states(scr)
    _reset(states)

    def step(jb, masks):
        k0 = pl.multiple_of(jb * DIFF_KB, DIFF_KB)
        kblk = k_ref[0, pl.ds(k0, DIFF_KB), :]
        vblk = v_ref[0, pl.ds(k0, DIFF_KB), :]
        active = sorted(masks)
        _online_block([functools.partial(_dot_nt, q2[c * CHAIN:(c + 1) * CHAIN], kblk) for c in active],
                      [masks[c] for c in active], vblk, [states[c] for c in active])

    n_sub = DIFF_TQ // CHAIN
    ratio = DIFF_TQ // DIFF_KB

    def body(jb, carry):
        step(jb, {c: None for c in range(2 * n_sub)})
        return carry

    lax.fori_loop(0, i * ratio, body, 0)
    for s in range(ratio):
        masks = {}
        for c in range(2 * n_sub):
            shift = (c % n_sub) * CHAIN - s * DIFF_KB
            if shift + CHAIN - 1 < 0:
                continue
            masks[c] = None if shift >= DIFF_KB - 1 else (
                lax.broadcasted_iota(jnp.int32, (CHAIN, DIFF_KB), 1)
                <= lax.broadcasted_iota(jnp.int32, (CHAIN, DIFF_KB), 0) + shift)
        step(i * ratio + s, masks)
    a = _normalized(states)
    lp = lam_ref[...]
    lam_init = init_ref[...]
    lam = (jnp.exp(jnp.sum(lp[0:1] * lp[1:2], axis=1, keepdims=True))
           - jnp.exp(jnp.sum(lp[2:3] * lp[3:4], axis=1, keepdims=True)) + lam_init)
    o = a[:DIFF_TQ] - lam * a[DIFF_TQ:]
    o = o * lax.rsqrt(jnp.mean(jnp.square(o), axis=1, keepdims=True) + RMS_EPS) * g_ref[...]
    o_ref[0] = (o * (1.0 - lam_init)).astype(BF16)


SB_TQ = 512
SB_KB = 256


def _sb_kernel(q_ref, k_ref, v_ref, tri_ref, o_ref, *scr):
    i = pl.program_id(2)
    scale = HEAD_DIM ** -0.5
    q = q_ref[0]
    states = [tuple(scr[2 * c:2 * c + 2]) for c in range(SB_TQ // CHAIN)]
    for run_scr, acc_scr in states:
        run_scr[...] = jnp.zeros(run_scr.shape, F32)
        acc_scr[...] = jnp.zeros(acc_scr.shape, F32)

    def step(jb, masks):
        k0 = pl.multiple_of(jb * SB_KB, SB_KB)
        kblk = k_ref[0, pl.ds(k0, SB_KB), :]
        vblk = v_ref[0, pl.ds(k0, SB_KB), :]
        active = sorted(masks)
        zs = [_dot_nt(q[c * CHAIN:(c + 1) * CHAIN], kblk) * scale for c in active]
        log_beta, log_1ms, splits = [], [], []
        for c, z in zip(active, zs):
            softplus = jnp.maximum(z, 0.0) + jnp.log(1.0 + jnp.exp(-jnp.abs(z)))
            log_1m = -softplus
            if masks[c] is not None:
                log_1m = jnp.where(masks[c], log_1m, 0.0)
            log_beta.append(z - softplus)
            log_1ms.append(log_1m)
            splits.append(jnp.concatenate(_split_bf16(log_1m), axis=0))
        tails = [_dot(hl, tri_ref[...]) for hl in splits]
        weights = []
        for c, lb, log_1m, both in zip(active, log_beta, log_1ms, tails):
            run_scr = states[c][0]
            tail = both[:CHAIN] + both[CHAIN:]
            run = run_scr[...]
            a = jnp.exp(lb + tail + run)
            if masks[c] is not None:
                a = jnp.where(masks[c], a, 0.0)
            weights.append(a.astype(BF16))
            run_scr[...] = run + tail[:, 0:1] + log_1m[:, 0:1]
        for c, a in zip(active, weights):
            states[c][1][...] += _dot(a, vblk)

    ratio = SB_TQ // SB_KB
    for s in reversed(range(ratio)):
        masks = {}
        for c in range(SB_TQ // CHAIN):
            shift = c * CHAIN - s * SB_KB
            if shift + CHAIN - 1 <= 0:
                continue
            masks[c] = None if shift >= SB_KB else (
                lax.broadcasted_iota(jnp.int32, (CHAIN, SB_KB), 1)
                < lax.broadcasted_iota(jnp.int32, (CHAIN, SB_KB), 0) + shift)
        step(i * ratio + s, masks)

    def body(n, carry):
        step(i * ratio - 1 - n, {c: None for c in range(SB_TQ // CHAIN)})
        return carry

    lax.fori_loop(0, i * ratio, body, 0)
    o_ref[0] = jnp.concatenate([acc[...] for _, acc in states], axis=0).astype(BF16)


def _sb_consts():
    j = np.arange(SB_KB)
    return (jnp.asarray((j[:, None] > j[None, :]).astype(np.float32), BF16),)


def _layer_norm(y, g, b):
    mu = jnp.mean(y, axis=1, keepdims=True)
    yc = y - mu
    var = jnp.mean(jnp.square(yc), axis=1, keepdims=True)
    return yc * lax.rsqrt(var + LN_EPS) * g + b


def _outproj_kernel(oa, ob, oc, od, w_ref, x_ref, g_ref, b_ref, y_ref, ybf_ref):
    width = oa.shape[1]
    mix = _dot(oa[...], w_ref[0:width, :])
    for k, o in enumerate((ob, oc, od), start=1):
        mix = mix + _dot(o[...], w_ref[k * width:(k + 1) * width, :])
    y = _layer_norm(DEEPNORM_ALPHA * x_ref[...] + mix, g_ref[...], b_ref[...])
    y_ref[...] = y
    ybf_ref[...] = y.astype(BF16)


def _outproj_ln(o_parts, w_out, x, g, b):
    t, d = x.shape
    tm = 512
    width = o_parts[0].shape[1]
    row = lambda w: pl.BlockSpec((tm, w), lambda i: (i, 0))
    vec = pl.BlockSpec((1, d), lambda i: (0, 0))
    return pl.pallas_call(
        _outproj_kernel,
        grid=(t // tm,),
        in_specs=[row(width)] * 4 + [pl.BlockSpec(w_out.shape, lambda i: (0, 0)), row(d), vec, vec],
        out_specs=[row(d), row(d)],
        out_shape=[jax.ShapeDtypeStruct((t, d), F32), jax.ShapeDtypeStruct((t, d), BF16)],
        compiler_params=_params(1),
        name="out_proj_ln",
    )(*o_parts, w_out, x, g, b)


PEER_TT = 512


def _top_values(x, n):
    out = []
    for r in range(n):
        mx = jnp.max(x, axis=0, keepdims=True)
        out.append(mx)
        if r + 1 < n:
            x = jnp.where(x == mx, -jnp.inf, x)
    return out


def _peer_a_kernel(xt_ref, wq_ref, sk_ref, thr_ref, ea_ref, b_ref, eb_ref, q_scr, cand_scr):
    q_scr[...] = _dot(wq_ref[...], xt_ref[...]).astype(BF16)
    n_pairs = cand_scr.shape[0]
    n_top = PEER_TOPK + 1

    def head(h, carry):
        r0 = pl.multiple_of(h * 2 * PEER_NKEYS, 2 * PEER_NKEYS)
        a = _dot(sk_ref[2 * h], q_scr[pl.ds(r0, PEER_NKEYS), :])
        b = _dot(sk_ref[2 * h + 1], q_scr[pl.ds(r0 + PEER_NKEYS, PEER_NKEYS), :])
        top_a = _top_values(a, n_top)
        top_b = _top_values(b, n_top)
        r = 0
        for ia in range(n_top):
            for ib in range(n_top // (ia + 1)):
                cand_scr[r:r + 1, :] = top_a[ia] + top_b[ib]
                r += 1
        cand_scr[r:n_pairs, :] = jnp.full((n_pairs - r, cand_scr.shape[1]), -jnp.inf, F32)
        top_s = _top_values(cand_scr[...], n_top)
        z = jnp.ones_like(top_s[0])
        for t in top_s[1:PEER_TOPK]:
            z = z + jnp.exp(t - top_s[0])
        tau = 0.5 * (top_s[PEER_TOPK - 1] + top_s[PEER_TOPK])
        thr_ref[h] = tau - a
        ea_ref[h] = jnp.exp(a - top_a[0]) / z
        b_ref[h] = b
        eb_ref[h] = jnp.exp(b - top_b[0])
        return carry

    lax.fori_loop(0, PEER_HEADS, head, 0)


def _peer_a(x_t, wq_t, sub_keys):
    d, t = x_t.shape
    tt = PEER_TT
    big = pl.BlockSpec((PEER_HEADS, PEER_NKEYS, tt), lambda i: (0, 0, i))
    big_shape = jax.ShapeDtypeStruct((PEER_HEADS, PEER_NKEYS, t), F32)
    n_pairs = 56
    return pl.pallas_call(
        _peer_a_kernel,
        grid=(t // tt,),
        in_specs=[pl.BlockSpec((d, tt), lambda i: (0, i)),
                  pl.BlockSpec(wq_t.shape, lambda i: (0, 0)),
                  pl.BlockSpec(sub_keys.shape, lambda i: (0, 0, 0))],
        out_specs=[big, big, big, big],
        out_shape=[big_shape] * 4,
        scratch_shapes=[pltpu.VMEM((wq_t.shape[0], tt), BF16), pltpu.VMEM((n_pairs, tt), F32)],
        compiler_params=_params(1),
        name="peer_scores_topk",
    )(x_t, wq_t, sub_keys)


PEER_ET = 1024
PEER_TC = 256


def _peer_b_kernel(xt_ref, u_ref, vt_ref, thr_ref, ea_ref, b_ref, eb_ref, o_ref, acc_scr, w_scr):
    e = pl.program_id(1)
    n_g = PEER_ET // PEER_NKEYS
    chunks = [slice(c * PEER_TC, (c + 1) * PEER_TC) for c in range(xt_ref.shape[1] // PEER_TC)]

    @pl.when(e == 0)
    def _():
        acc_scr[...] = jnp.zeros(acc_scr.shape, F32)

    hidden_all = _dot(u_ref[...], xt_ref[...])
    hidden = [hidden_all[:, cs] for cs in chunks]
    for g in range(n_g):
        i1 = e * n_g + g
        thr_rows = [thr_ref[h, pl.ds(i1, 1), :] for h in range(PEER_HEADS)]
        ea_rows = [ea_ref[h, pl.ds(i1, 1), :] for h in range(PEER_HEADS)]
        for c in range(xt_ref.shape[1] // LANES):
            ls = slice(c * LANES, (c + 1) * LANES)
            w = jnp.zeros((PEER_NKEYS, LANES), F32)
            for h in range(PEER_HEADS):
                w = w + jnp.where(b_ref[h, :, ls] >= thr_rows[h][:, ls], ea_rows[h][:, ls] * eb_ref[h, :, ls], 0.0)
            w_scr[g * PEER_NKEYS:(g + 1) * PEER_NKEYS, ls] = w
    for cs, hid in zip(chunks, hidden):
        p = (w_scr[:, cs] * jax.nn.gelu(hid)).astype(BF16)
        acc_scr[:, cs] += _dot(vt_ref[...], p)

    @pl.when(e == pl.num_programs(1) - 1)
    def _():
        o_ref[...] = acc_scr[...].T


def _peer_b(x_t, u, v_t, stage_a):
    d, t = x_t.shape
    n_exp = u.shape[0]
    tt = PEER_TT
    big = pl.BlockSpec((PEER_HEADS, PEER_NKEYS, tt), lambda i, e: (0, 0, i))
    return pl.pallas_call(
        _peer_b_kernel,
        grid=(t // tt, n_exp // PEER_ET),
        in_specs=[pl.BlockSpec((d, tt), lambda i, e: (0, i)),
                  pl.BlockSpec((PEER_ET, d), lambda i, e: (e, 0)),
                  pl.BlockSpec((d, PEER_ET), lambda i, e: (0, e)),
                  big, big, big, big],
        out_specs=pl.BlockSpec((tt, d), lambda i, e: (i, 0)),
        out_shape=jax.ShapeDtypeStruct((t, d), F32),
        scratch_shapes=[pltpu.VMEM((d, tt), F32), pltpu.VMEM((PEER_ET, tt), F32)],
        compiler_params=_params(2),
        name="peer_experts",
    )(x_t, u, v_t, *stage_a)


def _resid_ln_kernel(x_ref, f_ref, g_ref, b_ref, y_ref, ybf_ref):
    y = _layer_norm(DEEPNORM_ALPHA * x_ref[...] + f_ref[...], g_ref[...], b_ref[...])
    y_ref[...] = y
    ybf_ref[...] = y.astype(BF16)


def _resid_ln(x, f, g, b):
    t, d = x.shape
    tm = 512
    row = pl.BlockSpec((tm, d), lambda i: (i, 0))
    vec = pl.BlockSpec((1, d), lambda i: (0, 0))
    return pl.pallas_call(
        _resid_ln_kernel,
        grid=(t // tm,),
        in_specs=[row, row, vec, vec],
        out_specs=[row, row],
        out_shape=[jax.ShapeDtypeStruct((t, d), F32), jax.ShapeDtypeStruct((t, d), BF16)],
        compiler_params=_params(1),
        name="resid_ln",
    )(x, f, g, b)


def _mixing_heads(p3, kvc, lam_params, diff_g, lam_init, consts):
    nsa_c, dil_c, sb_c = consts
    o_a = _nsa_attention(p3, kvc, nsa_c)
    o_b = _head_attention(_dil_kernel, "dilated_attention", p3, G_BQ, G_BK, G_BV, DIL_TQ, dil_c,
                          _softmax_scratch(DIL_TQ))
    diff_consts = (lam_params, diff_g.reshape(1, HEAD_DIM), jnp.full((1, 1), lam_init, F32))
    o_c = _head_attention(_diff_kernel, "diff_attention", p3, G_CQ, G_CK, G_CV, DIFF_TQ, diff_consts,
                          _softmax_scratch(2 * DIFF_TQ))
    o_d = _head_attention(_sb_kernel, "stick_breaking_attention", p3, G_DQ, G_DK, G_DV, SB_TQ, sb_c,
                          [pltpu.VMEM((CHAIN, 1), F32), pltpu.VMEM((CHAIN, LANES), F32)] * (SB_TQ // CHAIN))
    return o_a, o_b, o_c, o_d


def kernel(x, w_in, w_out, nsa_cmp_pe, nsa_cmp_w1, nsa_cmp_w2, diff_lambda, diff_norm_g, ln1_g, ln1_b,
           peer_wq, peer_subkeys, peer_u, peer_v, ln2_g, ln2_b):
    b, seq, d = x.shape
    t = b * seq
    depth = w_in.shape[0]

    w_in_p = _prep_w_in(w_in)
    w_out_bf = w_out.astype(BF16)
    cmp_w1 = nsa_cmp_w1.astype(BF16)
    cmp_w2 = nsa_cmp_w2.astype(BF16)
    cmp_pe = jnp.broadcast_to(nsa_cmp_pe.reshape(depth, 2, 1, CMP_LEN * HEAD_DIM),
                              (depth, 2, 8, CMP_LEN * HEAD_DIM)).astype(BF16)
    wq_t = jnp.swapaxes(peer_wq, 1, 2).astype(BF16)
    sub_keys = peer_subkeys.reshape(depth, 2 * PEER_HEADS, PEER_NKEYS, -1).astype(BF16)
    u_bf = peer_u.astype(BF16)
    v_t = jnp.swapaxes(peer_v, 1, 2).astype(BF16)

    tables = _rope_tables(seq)
    consts = (_nsa_consts(seq), _dil_consts(), _sb_consts())

    xf = x.reshape(t, d)
    x_bf = xf.astype(BF16)
    for l in range(depth):
        lam_init = 0.8 - 0.6 * math.exp(-0.3 * l)
        proj = _project(x_bf, w_in_p[l], tables, seq)
        p3 = proj.reshape(b, seq, N_GROUPS * LANES)
        chunks = jnp.stack([p3[:, :, G_AKC * LANES:(G_AKC + 1) * LANES],
                            p3[:, :, G_AVC * LANES:(G_AVC + 1) * LANES]], axis=1)
        chunks = chunks.reshape(b, 2, seq // CMP_STRIDE, CMP_STRIDE * HEAD_DIM)
        kvc = _compress(chunks, cmp_w1[l], cmp_pe[l], cmp_w2[l])
        heads = _mixing_heads(p3, kvc, diff_lambda[l], diff_norm_g[l], lam_init, consts)
        heads = [o.reshape(t, -1) for o in heads]
        xf, x_bf = _outproj_ln(heads, w_out_bf[l], xf, ln1_g[l].reshape(1, d), ln1_b[l].reshape(1, d))
        x_t = x_bf.T
        stage_a = _peer_a(x_t, wq_t[l], sub_keys[l])
        ffn = _peer_b(x_t, u_bf[l], v_t[l], stage_a)
        xf, x_bf = _resid_ln(xf, ffn, ln2_g[l].reshape(1, d), ln2_b[l].reshape(1, d))
    return xf.reshape(b, seq, d)
```
